```python
import math, functools
import jax, jax.numpy as jnp
from jax import lax
import numpy as np

D_MODEL = 1024
BATCH = 2
SEQ = 8192
DEPTH = 2
DEC_BATCH = 128
DEC_SEQ = 1
PAST_LEN = 2048
PAGE_SIZE = 128

N_A_LAYERS = DEPTH // 2
N_B_LAYERS = DEPTH - N_A_LAYERS
CHUNK = 128
A_WIDTH = D_MODEL
A_GROUPS = 8
A_GROUP_DIM = A_WIDTH // A_GROUPS
N_HEADS = 16
HEAD_DIM = D_MODEL // N_HEADS
KV_HEADS = 4
HEADS_PER_GROUP = N_HEADS // KV_HEADS
CMP_STRIDE = 16
CMP_BLOCK = 2 * CMP_STRIDE
CMP_HIDDEN = 4 * HEAD_DIM
SLC_BLOCK = 64
N_SELECT = 16
WINDOW = 512
Q_BLOCK = 128
D_FF = ((8 * D_MODEL + 3 * 256 - 1) // (3 * 256)) * 256
ROPE_THETA = 10000.0
EPS = 1e-6
NEG_INF = -1e30
FORCE_SCORE = 1e9
TINY = 1e-30

kernel_name = 'yoco_gmlp_nsa_decoder_step'


def rms_norm(x, g):
    xf = x.astype(jnp.float32)
    y = xf * lax.rsqrt(jnp.mean(xf * xf, axis=-1, keepdims=True) + EPS)
    return (y * g.astype(jnp.float32)).astype(x.dtype)


def rope(x, pos):
    half = HEAD_DIM // 2
    inv_freq = ROPE_THETA ** (-jnp.arange(half, dtype=jnp.float32) / half)
    ang = pos.astype(jnp.float32)[:, None] * inv_freq[None, :]
    cos = jnp.cos(ang)[None, :, None, :]
    sin = jnp.sin(ang)[None, :, None, :]
    xf = x.astype(jnp.float32)
    x1, x2 = xf[..., :half], xf[..., half:]
    return jnp.concatenate([x1 * cos - x2 * sin, x2 * cos + x1 * sin], axis=-1).astype(x.dtype)


def ada_modulation(c, w_ada, b_ada):
    m = jax.nn.silu(c) @ w_ada + b_ada
    return jnp.split(m[:, None, :], 6, axis=-1)


def masked_softmax(s, mask):
    s = jnp.where(mask, s.astype(jnp.float32), NEG_INF)
    m = jnp.max(s, axis=-1, keepdims=True)
    p = jnp.where(mask, jnp.exp(s - m), 0.0)
    return p / jnp.maximum(jnp.sum(p, axis=-1, keepdims=True), TINY)


def swiglu(h, w_gu, w_down):
    g, u = jnp.split(h @ w_gu, 2, axis=-1)
    return (jax.nn.silu(g) * u) @ w_down


def chunk_token_mix(v, w_s, b_s):
    B, S = v.shape[:2]
    n_chunks = -(-S // CHUNK)
    vp = jnp.pad(v, ((0, 0), (0, n_chunks * CHUNK - S), (0, 0), (0, 0)))
    vp = vp.reshape(B, n_chunks, CHUNK, A_GROUPS, A_GROUP_DIM)
    causal = jnp.tril(jnp.ones((CHUNK, CHUNK), dtype=bool))
    ws = jnp.where(causal[None], w_s, 0.0)
    mixed = jnp.einsum('gts,bnsgd->bntgd', ws, vp) + b_s.T[None, None, :, :, None]
    return mixed.reshape(B, n_chunks * CHUNK, A_GROUPS, A_GROUP_DIM)[:, :S]


def mixer_a(h, w_in, v_gain, w_s, b_s, w_out):
    B, S = h.shape[:2]
    u, v = jnp.split(jax.nn.gelu(h @ w_in), 2, axis=-1)
    v = rms_norm(v, v_gain)
    mixed = chunk_token_mix(v.reshape(B, S, A_GROUPS, A_GROUP_DIM), w_s, b_s).reshape(B, S, A_WIDTH)
    return (u * mixed) @ w_out, v


def project_kv(x, pos, kv_gain, w_kv, k_gains):
    B, S = x.shape[:2]
    kv = (rms_norm(x, kv_gain) @ w_kv).reshape(B, S, 6, KV_HEADS, HEAD_DIM)
    k_cmp = kv[:, :, 0]
    v_cmp = kv[:, :, 1]
    k_slc = rope(rms_norm(kv[:, :, 2], k_gains[1]), pos)
    v_slc = kv[:, :, 3]
    k_win = rope(rms_norm(kv[:, :, 4], k_gains[2]), pos)
    v_win = kv[:, :, 5]
    return k_cmp, v_cmp, k_slc, v_slc, k_win, v_win


def compress(rows, pe, w1, b1, w2, b2):
    B, T = rows.shape[:2]
    n_half = T // CMP_STRIDE
    r = rows[:, :n_half * CMP_STRIDE].reshape(B, n_half, CMP_STRIDE, KV_HEADS, HEAD_DIM)
    blocks = jnp.concatenate([r[:, :-1], r[:, 1:]], axis=2) + pe[None, None, :, None, :]
    flat = blocks.transpose(0, 1, 3, 2, 4).reshape(B, n_half - 1, KV_HEADS, CMP_BLOCK * HEAD_DIM)
    return jax.nn.gelu(flat @ w1 + b1) @ w2 + b2


def to_select_blocks(rows):
    B, T = rows.shape[:2]
    ns = -(-T // SLC_BLOCK)
    r = jnp.pad(rows, ((0, 0), (0, ns * SLC_BLOCK - T), (0, 0), (0, 0)))
    return r.reshape(B, ns, SLC_BLOCK, KV_HEADS, HEAD_DIM).transpose(0, 3, 1, 2, 4)


def build_b_side(k_cmp, v_cmp, k_slc, v_slc, k_gains, cmp_pe, cmp_w1, cmp_b1, cmp_w2, cmp_b2):
    kc = rms_norm(compress(k_cmp, cmp_pe[0], cmp_w1[0], cmp_b1[0], cmp_w2[0], cmp_b2[0]), k_gains[0])
    vc = compress(v_cmp, cmp_pe[1], cmp_w1[1], cmp_b1[1], cmp_w2[1], cmp_b2[1])
    return (kc, vc, to_select_blocks(k_slc), to_select_blocks(v_slc))


def nsa_queries(h, pos, w_in, q_gain):
    B, S = h.shape[:2]
    proj = h @ w_in
    q = rms_norm(proj[..., :N_HEADS * HEAD_DIM].reshape(B, S, N_HEADS, HEAD_DIM), q_gain)
    gates = jax.nn.sigmoid(proj[..., N_HEADS * HEAD_DIM:].astype(jnp.float32))
    gates = gates.reshape(B, S, N_HEADS, 3).astype(h.dtype)
    return q, rope(q, pos), gates


def nsa_core(q, q_rot, qpos, kc, vc, ksb, vsb, kw, vw, wpos, gates):
    B, Q = q.shape[:2]
    NC = kc.shape[1]
    NS = ksb.shape[2]
    scale = HEAD_DIM ** -0.5
    qg = q.reshape(B, Q, KV_HEADS, HEADS_PER_GROUP, HEAD_DIM)
    qrg = q_rot.reshape(B, Q, KV_HEADS, HEADS_PER_GROUP, HEAD_DIM)
    cmp_start = jnp.arange(NC, dtype=jnp.int32) * CMP_STRIDE
    mask_c = (cmp_start + CMP_BLOCK - 1)[None, :] <= qpos[:, None]
    s_c = jnp.einsum('bqghd,bngd->bqghn', qg, kc) * scale
    p_c = masked_softmax(s_c, mask_c[None, :, None, None, :])
    o_c = jnp.einsum('bqghn,bngd->bqghd', p_c.astype(vc.dtype), vc)
    slc_start = jnp.arange(NS, dtype=jnp.int32) * SLC_BLOCK
    overlap = ((cmp_start[:, None] < slc_start[None, :] + SLC_BLOCK)
               & (cmp_start[:, None] + CMP_BLOCK > slc_start[None, :])).astype(jnp.float32)
    imp = jnp.einsum('bqghn,ns->bqgs', p_c, overlap)
    blk = jnp.arange(NS, dtype=jnp.int32)[None, :]
    cur = (qpos // SLC_BLOCK)[:, None]
    valid = (blk <= cur)[None, :, None, :]
    forced = ((blk == 0) | (blk == cur) | (blk == cur - 1))[None, :, None, :]
    score = jnp.where(valid & forced, FORCE_SCORE, jnp.where(valid, imp, NEG_INF))
    _, idx = lax.top_k(score, min(N_SELECT, NS))
    n_sel = idx.shape[-1]
    bi = jnp.arange(B)[:, None, None, None]
    gi = jnp.arange(KV_HEADS)[None, None, :, None]
    ks = ksb[bi, gi, idx].reshape(B, Q, KV_HEADS, n_sel * SLC_BLOCK, HEAD_DIM)
    vs = vsb[bi, gi, idx].reshape(B, Q, KV_HEADS, n_sel * SLC_BLOCK, HEAD_DIM)
    kpos = (idx[..., None] * SLC_BLOCK + jnp.arange(SLC_BLOCK, dtype=jnp.int32)).reshape(
        B, Q, KV_HEADS, n_sel * SLC_BLOCK)
    mask_s = kpos <= qpos[None, :, None, None]
    s_s = jnp.einsum('bqghd,bqgkd->bqghk', qrg, ks) * scale
    p_s = masked_softmax(s_s, mask_s[:, :, :, None, :])
    o_s = jnp.einsum('bqghk,bqgkd->bqghd', p_s.astype(vs.dtype), vs)
    dpos = qpos[:, None] - wpos[None, :]
    mask_w = (dpos >= 0) & (dpos <= WINDOW) & (wpos[None, :] >= 0)
    s_w = jnp.einsum('bqghd,bkgd->bqghk', qrg, kw) * scale
    p_w = masked_softmax(s_w, mask_w[None, :, None, None, :])
    o_w = jnp.einsum('bqghk,bkgd->bqghd', p_w.astype(vw.dtype), vw)
    g = gates.reshape(B, Q, KV_HEADS, HEADS_PER_GROUP, 3)
    o = o_c * g[..., 0:1] + o_s * g[..., 1:2] + o_w * g[..., 2:3]
    return o.reshape(B, Q, N_HEADS * HEAD_DIM)


def run_group(x, c, make_kv_side, attend, w_ada, b_ada, norm_g, a_w_in, a_v_gain, a_w_s, a_b_s,
              a_w_out, ffn_w_gu, ffn_w_down):
    a_v_rows = []
    side = None
    new_rows = None
    for layer in range(DEPTH):
        shift1, scale1, gate1, shift2, scale2, gate2 = ada_modulation(c, w_ada[layer], b_ada[layer])
        if layer == N_A_LAYERS:
            side, new_rows = make_kv_side(x)
        h = rms_norm(x, norm_g[layer, 0]) * (1.0 + scale1) + shift1
        if layer < N_A_LAYERS:
            out, v = mixer_a(h, a_w_in[layer], a_v_gain[layer], a_w_s[layer], a_b_s[layer], a_w_out[layer])
            a_v_rows.append(v)
        else:
            out = attend(h, side, layer - N_A_LAYERS)
        x = x + gate1 * out
        h = rms_norm(x, norm_g[layer, 1]) * (1.0 + scale2) + shift2
        x = x + gate2 * swiglu(h, ffn_w_gu[layer], ffn_w_down[layer])
    return x, a_v_rows, new_rows


def setup_inputs(seed: int = 0) -> dict:
    key = jax.random.key(seed)
    k = jax.random.split(key, 32)
    f32 = jnp.float32

    def nrm(i, shape, scale=1.0):
        return scale * jax.random.normal(k[i], shape, f32)

    n_pages = PAST_LEN // PAGE_SIZE
    n_used = DEC_BATCH * n_pages
    n_phys = n_used + n_used // 4
    page_table = jax.random.permutation(k[8], n_phys)[:n_used].reshape(DEC_BATCH, n_pages).astype(jnp.int32)
    win_buf = min(WINDOW, PAST_LEN)
    cache_shape = (n_phys, PAGE_SIZE, KV_HEADS, HEAD_DIM)
    win_shape = (DEC_BATCH, win_buf, KV_HEADS, HEAD_DIM)
    return {
        'x_prompt': nrm(0, (BATCH, SEQ, D_MODEL)),
        'x_sample': nrm(1, (DEC_BATCH, DEC_SEQ, D_MODEL)),
        'cache_cmp_k': nrm(2, cache_shape),
        'cache_cmp_v': nrm(3, cache_shape),
        'cache_slc_k': nrm(4, cache_shape),
        'cache_slc_v': nrm(5, cache_shape),
        'state_win_k': nrm(6, win_shape),
        'state_win_v': nrm(7, win_shape),
        'page_table': page_table,
        'c_prompt': nrm(9, (BATCH, D_MODEL)),
        'c_sample': nrm(10, (DEC_BATCH, D_MODEL)),
        'w_ada': nrm(11, (DEPTH, D_MODEL, 6 * D_MODEL), D_MODEL ** -0.5),
        'b_ada': nrm(12, (DEPTH, 6 * D_MODEL), 0.01),
        'norm_g': 1.0 + nrm(13, (DEPTH, 2, D_MODEL), 0.02),
        'a_w_in': nrm(14, (N_A_LAYERS, D_MODEL, 2 * A_WIDTH), D_MODEL ** -0.5),
        'a_v_gain': 1.0 + nrm(15, (N_A_LAYERS, A_WIDTH), 0.02),
        'a_w_s': nrm(16, (N_A_LAYERS, A_GROUPS, CHUNK, CHUNK), CHUNK ** -0.5),
        'a_b_s': 1.0 + nrm(17, (N_A_LAYERS, A_GROUPS, CHUNK), 0.02),
        'a_w_out': nrm(18, (N_A_LAYERS, A_WIDTH, D_MODEL), A_WIDTH ** -0.5),
        'b_w_in': nrm(19, (N_B_LAYERS, D_MODEL, N_HEADS * HEAD_DIM + 3 * N_HEADS), D_MODEL ** -0.5),
        'b_q_gain': 1.0 + nrm(20, (N_B_LAYERS, HEAD_DIM), 0.02),
        'b_w_out': nrm(21, (N_B_LAYERS, N_HEADS * HEAD_DIM, D_MODEL), (N_HEADS * HEAD_DIM) ** -0.5),
        'kv_gain': 1.0 + nrm(22, (D_MODEL,), 0.02),
        'w_kv': nrm(23, (D_MODEL, 6 * KV_HEADS * HEAD_DIM), D_MODEL ** -0.5),
        'k_gains': 1.0 + nrm(24, (3, HEAD_DIM), 0.02),
        'cmp_pe': nrm(25, (2, CMP_BLOCK, HEAD_DIM), 0.1),
        'cmp_w1': nrm(26, (2, CMP_BLOCK * HEAD_DIM, CMP_HIDDEN), (CMP_BLOCK * HEAD_DIM) ** -0.5),
        'cmp_b1': nrm(27, (2, CMP_HIDDEN), 0.01),
        'cmp_w2': nrm(28, (2, CMP_HIDDEN, HEAD_DIM), CMP_HIDDEN ** -0.5),
        'cmp_b2': nrm(29, (2, HEAD_DIM), 0.01),
        'ffn_w_gu': nrm(30, (DEPTH, D_MODEL, 2 * D_FF), D_MODEL ** -0.5),
        'ffn_w_down': nrm(31, (DEPTH, D_FF, D_MODEL), D_FF ** -0.5),
    }


def reference(x_prompt, x_sample, cache_cmp_k, cache_cmp_v, cache_slc_k, cache_slc_v, state_win_k,
              state_win_v, page_table, c_prompt, c_sample, w_ada, b_ada, norm_g, a_w_in, a_v_gain,
              a_w_s, a_b_s, a_w_out, b_w_in, b_q_gain, b_w_out, kv_gain, w_kv, k_gains, cmp_pe,
              cmp_w1, cmp_b1, cmp_w2, cmp_b2, ffn_w_gu, ffn_w_down):
    cmp_params = (k_gains, cmp_pe, cmp_w1, cmp_b1, cmp_w2, cmp_b2)

    pos_p = jnp.arange(SEQ, dtype=jnp.int32)

    def kv_side_prompt(xm):
        k_cmp, v_cmp, k_slc, v_slc, k_win, v_win = project_kv(xm, pos_p, kv_gain, w_kv, k_gains)
        side = build_b_side(k_cmp, v_cmp, k_slc, v_slc, *cmp_params)
        pad = ((0, 0), (WINDOW, 0), (0, 0), (0, 0))
        keep = min(WINDOW, SEQ)
        rows = (k_cmp, v_cmp, k_slc, v_slc, k_win[:, SEQ - keep:], v_win[:, SEQ - keep:])
        return side + (jnp.pad(k_win, pad), jnp.pad(v_win, pad)), rows

    def attend_prompt(h, side, j):
        kc, vc, ksb, vsb, kwp, vwp = side
        B = h.shape[0]
        nq = SEQ // Q_BLOCK
        q, q_rot, gates = nsa_queries(h, pos_p, b_w_in[j], b_q_gain[j])

        def to_blocks(t):
            return jnp.moveaxis(t.reshape(B, nq, Q_BLOCK, *t.shape[2:]), 1, 0)

        def one_block(args):
            n, qb, qrb, gb = args
            start = n * Q_BLOCK
            qpos = start + jnp.arange(Q_BLOCK, dtype=jnp.int32)
            wpos = start - WINDOW + jnp.arange(WINDOW + Q_BLOCK, dtype=jnp.int32)
            kw = lax.dynamic_slice_in_dim(kwp, start, WINDOW + Q_BLOCK, axis=1)
            vw = lax.dynamic_slice_in_dim(vwp, start, WINDOW + Q_BLOCK, axis=1)
            return nsa_core(qb, qrb, qpos, kc, vc, ksb, vsb, kw, vw, wpos, gb)

        o = lax.map(one_block, (jnp.arange(nq, dtype=jnp.int32), to_blocks(q), to_blocks(q_rot),
                                to_blocks(gates)))
        o = jnp.moveaxis(o, 0, 1).reshape(B, SEQ, N_HEADS * HEAD_DIM)
        return o @ b_w_out[j]

    y_prompt, a_rows_p, rows_p = run_group(x_prompt, c_prompt, kv_side_prompt, attend_prompt, w_ada,
                                           b_ada, norm_g, a_w_in, a_v_gain, a_w_s, a_b_s, a_w_out,
                                           ffn_w_gu, ffn_w_down)

    pos_s = PAST_LEN + jnp.arange(DEC_SEQ, dtype=jnp.int32)
    n_dec = x_sample.shape[0]
    win_buf = state_win_k.shape[1]

    def gather_past(cache):
        return cache[page_table].reshape(n_dec, PAST_LEN, KV_HEADS, HEAD_DIM)

    def kv_side_sample(xm):
        k_cmp_n, v_cmp_n, k_slc_n, v_slc_n, k_win_n, v_win_n = project_kv(xm, pos_s, kv_gain, w_kv, k_gains)
        k_cmp = jnp.concatenate([gather_past(cache_cmp_k), k_cmp_n], axis=1)
        v_cmp = jnp.concatenate([gather_past(cache_cmp_v), v_cmp_n], axis=1)
        k_slc = jnp.concatenate([gather_past(cache_slc_k), k_slc_n], axis=1)
        v_slc = jnp.concatenate([gather_past(cache_slc_v), v_slc_n], axis=1)
        side = build_b_side(k_cmp, v_cmp, k_slc, v_slc, *cmp_params)
        kw = jnp.concatenate([state_win_k, k_win_n], axis=1)
        vw = jnp.concatenate([state_win_v, v_win_n], axis=1)
        keep = min(WINDOW, PAST_LEN + DEC_SEQ)
        total = kw.shape[1]
        rows = (k_cmp_n, v_cmp_n, k_slc_n, v_slc_n, kw[:, total - keep:], vw[:, total - keep:])
        return side + (kw, vw), rows

    def attend_sample(h, side, j):
        kc, vc, ksb, vsb, kw, vw = side
        q, q_rot, gates = nsa_queries(h, pos_s, b_w_in[j], b_q_gain[j])
        wpos = PAST_LEN - win_buf + jnp.arange(win_buf + DEC_SEQ, dtype=jnp.int32)
        o = nsa_core(q, q_rot, pos_s, kc, vc, ksb, vsb, kw, vw, wpos, gates)
        return o @ b_w_out[j]

    y_sample, a_rows_s, rows_s = run_group(x_sample, c_sample, kv_side_sample, attend_sample, w_ada,
                                           b_ada, norm_g, a_w_in, a_v_gain, a_w_s, a_b_s, a_w_out,
                                           ffn_w_gu, ffn_w_down)

    last_chunk_start = ((SEQ - 1) // CHUNK) * CHUNK
    a_v_prompt = jnp.stack([v[:, last_chunk_start:] for v in a_rows_p], axis=0)
    a_v_sample = jnp.stack(a_rows_s, axis=0)
    cmp_k_prompt, cmp_v_prompt, slc_k_prompt, slc_v_prompt, win_k_prompt, win_v_prompt = rows_p
    cmp_k_sample, cmp_v_sample, slc_k_sample, slc_v_sample, win_k_sample, win_v_sample = rows_s
    return (y_prompt, y_sample, a_v_prompt, a_v_sample,
            cmp_k_prompt, cmp_v_prompt, slc_k_prompt, slc_v_prompt, win_k_prompt, win_v_prompt,
            cmp_k_sample, cmp_v_sample, slc_k_sample, slc_v_sample, win_k_sample, win_v_sample)
```

```python
import functools

import numpy as np
import jax
import jax.numpy as jnp
from jax import lax
from jax.experimental import pallas as pl
from jax.experimental.pallas import tpu as pltpu

F32 = jnp.float32
BF16 = jnp.bfloat16

EPS = 1e-6
NEG_INF = -1e30
FORCE_SCORE = 1e9
TINY = 1e-30
ROPE_THETA = 10000.0

N_HEADS = 16
HEAD_DIM = 64
KV_HEADS = 4
HEADS_PER_GROUP = N_HEADS // KV_HEADS
KV_WIDTH = KV_HEADS * HEAD_DIM
CHUNK = 128
A_GROUPS = 8
CMP_STRIDE = 16
CMP_BLOCK = 2 * CMP_STRIDE
SLC_BLOCK = 64
N_SELECT = 16
WINDOW = 512
Q_BLOCK = 128
LANES = 128
GATE_PAD = 128
VMEM_LIMIT = 56 * 1024 * 1024

_NT = (((1,), (1,)), ((), ()))


def _params(n_grid):
    return pltpu.CompilerParams(dimension_semantics=("arbitrary",) * n_grid, vmem_limit_bytes=VMEM_LIMIT)


def _dot(a, b):
    return jnp.dot(a, b, preferred_element_type=F32)


def _dot_nt(a, b):
    return lax.dot_general(a, b, _NT, preferred_element_type=F32)


def _rms(x, g):
    return x * lax.rsqrt(jnp.mean(x * x, axis=-1, keepdims=True) + EPS) * g


def _split_dot(x, w):
    hi = x.astype(BF16)
    lo = (x - hi.astype(F32)).astype(BF16)
    return _dot(hi, w) + _dot(lo, w)


def _iota(shape, dim):
    return lax.broadcasted_iota(jnp.int32, shape, dim)


def _const_spec(shape):
    nd = len(shape)
    return pl.BlockSpec(shape, lambda *_: (0,) * nd)


def _mod_spec(m, tm):
    d = m.shape[-1]
    if m.shape[1] == 1:
        return pl.BlockSpec((1, 1, d), lambda b, j: (b, 0, 0))
    return pl.BlockSpec((1, tm, d), lambda b, j: (b, j, 0))


def _ada_kernel(c_ref, w_ref, b_ref, o_ref):
    c = c_ref[...]
    s = (c * jax.nn.sigmoid(c)).astype(BF16)
    o_ref[0] = _dot(s, w_ref[0].astype(BF16)) + b_ref[0]


def _ada(c_all, w_ada, b_ada, tn=1536):
    n_layers, d, d6 = w_ada.shape
    m = c_all.shape[0]
    return pl.pallas_call(
        _ada_kernel,
        grid=(n_layers, d6 // tn),
        in_specs=[pl.BlockSpec((m, d), lambda l, j: (0, 0)),
                  pl.BlockSpec((1, d, tn), lambda l, j: (l, 0, j)),
                  pl.BlockSpec((1, 1, tn), lambda l, j: (l, 0, j))],
        out_specs=pl.BlockSpec((1, m, tn), lambda l, j: (l, 0, j)),
        out_shape=jax.ShapeDtypeStruct((n_layers, m, d6), F32),
        compiler_params=_params(2),
        name="ada_modulation",
    )(c_all, w_ada, b_ada.reshape(n_layers, 1, d6))


def _mixer_a_kernel(x_ref, sh_ref, sc_ref, gt_ref, ng_ref, win_ref, vg_ref, ws_ref, bias_ref, wout_ref,
                    x1_ref, v_ref, um_ref, *, single_token):
    x = x_ref[0]
    h = _rms(x, ng_ref[...]) * (1.0 + sc_ref[0]) + sh_ref[0]
    act = jax.nn.gelu(_dot(h.astype(BF16), win_ref[...]))
    width = act.shape[1] // 2
    u = act[:, :width]
    v = _rms(act[:, width:], vg_ref[...])
    if single_token:
        v_ref[0] = v
        um_ref[...] = (u * (v * ws_ref[...] + bias_ref[...])).astype(BF16)
    else:
        tm = x.shape[0]
        gd = width // A_GROUPS

        @pl.when(pl.program_id(1) == pl.num_programs(1) - 1)
        def _():
            v_ref[0] = v[tm - CHUNK:, :]

        causal = _iota((CHUNK, CHUNK), 0) >= _iota((CHUNK, CHUNK), 1)
        for g in range(A_GROUPS):
            wg = jnp.where(causal, ws_ref[g], 0.0).astype(BF16)
            cs = slice(g * gd, (g + 1) * gd)
            for c in range(tm // CHUNK):
                rs = slice(c * CHUNK, (c + 1) * CHUNK)
                mixed = _dot(wg, v[rs, cs].astype(BF16)) + bias_ref[:, cs]
                um_ref[rs, cs] = (u[rs, cs] * mixed).astype(BF16)
    x1_ref[0] = x + gt_ref[0] * _dot(um_ref[...], wout_ref[...])


def _mixer_a(x, sh, sc, gt, ng, w_in, v_gain, ws, bias, w_out, *, single_token, tm):
    bm, sm, d = x.shape
    width = w_out.shape[0]
    v_rows = sm if single_token else CHUNK
    kern = functools.partial(_mixer_a_kernel, single_token=single_token)
    return pl.pallas_call(
        kern,
        grid=(bm, sm // tm),
        in_specs=[pl.BlockSpec((1, tm, d), lambda b, j: (b, j, 0)),
                  _mod_spec(sh, tm), _mod_spec(sc, tm), _mod_spec(gt, tm),
                  _const_spec(ng.shape), _const_spec(w_in.shape), _const_spec(v_gain.shape),
                  _const_spec(ws.shape), _const_spec(bias.shape), _const_spec(w_out.shape)],
        out_specs=[pl.BlockSpec((1, tm, d), lambda b, j: (b, j, 0)),
                   pl.BlockSpec((1, v_rows, width), lambda b, j: (b, 0, 0))],
        out_shape=[jax.ShapeDtypeStruct((bm, sm, d), F32),
                   jax.ShapeDtypeStruct((bm, v_rows, width), F32)],
        scratch_shapes=[pltpu.VMEM((tm, width), BF16)],
        compiler_params=_params(2),
        name="mixer_a",
    )(x, sh, sc, gt, ng, w_in, v_gain, ws, bias, w_out)


def _ffn_kernel(x_ref, sh_ref, sc_ref, gt_ref, ng_ref, wgu_ref, wd_ref, o_ref, act_ref, *, fc):
    x = x_ref[0]
    h = (_rms(x, ng_ref[...]) * (1.0 + sc_ref[0]) + sh_ref[0]).astype(BF16)
    dff = wd_ref.shape[0]
    for j in range(dff // fc):
        g = _dot(h, wgu_ref[:, j * fc:(j + 1) * fc])
        u = _dot(h, wgu_ref[:, dff + j * fc:dff + (j + 1) * fc])
        act_ref[:, j * fc:(j + 1) * fc] = (g * jax.nn.sigmoid(g) * u).astype(BF16)
    o_ref[0] = x + gt_ref[0] * _dot(act_ref[...], wd_ref[...])


def _ffn(x, sh, sc, gt, ng, w_gu, w_down, *, tm, fc=256):
    bm, sm, d = x.shape
    dff = w_down.shape[0]
    assert dff % fc == 0
    return pl.pallas_call(
        functools.partial(_ffn_kernel, fc=fc),
        grid=(bm, sm // tm),
        in_specs=[pl.BlockSpec((1, tm, d), lambda b, j: (b, j, 0)),
                  _mod_spec(sh, tm), _mod_spec(sc, tm), _mod_spec(gt, tm),
                  _const_spec(ng.shape), _const_spec(w_gu.shape), _const_spec(w_down.shape)],
        out_specs=pl.BlockSpec((1, tm, d), lambda b, j: (b, j, 0)),
        out_shape=jax.ShapeDtypeStruct((bm, sm, d), F32),
        scratch_shapes=[pltpu.VMEM((tm, dff), BF16)],
        compiler_params=_params(2),
        name="swiglu_ffn",
    )(x, sh, sc, gt, ng, w_gu, w_down)


def _head_norm_rope(t, gain_col, cos, sin):
    rows, n = t.shape
    t3 = t.reshape(rows // HEAD_DIM, HEAD_DIM, n)
    y = t3 * lax.rsqrt(jnp.mean(t3 * t3, axis=1, keepdims=True) + EPS) * gain_col[None]
    half = HEAD_DIM // 2
    y1 = y[:, :half, :]
    y2 = y[:, half:, :]
    c = cos[None]
    s = sin[None]
    rot = jnp.concatenate([y1 * c - y2 * s, y2 * c + y1 * s], axis=1)
    return y.reshape(rows, n), rot.reshape(rows, n)


def _kv_proj_kernel(x_ref, kvg_ref, w_ref, g1_ref, g2_ref, cos_ref, sin_ref,
                    kc_ref, vc_ref, ks_ref, vs_ref, kw_ref, vw_ref, ksb_ref, vsb_ref, kwb_ref, vwb_ref):
    xn = _rms(x_ref[0], kvg_ref[...]).astype(BF16)
    kvt = _dot_nt(w_ref[...], xn)
    cos = cos_ref[...]
    sin = sin_ref[...]

    def part(i):
        return kvt[i * KV_WIDTH:(i + 1) * KV_WIDTH]

    kc_ref[0] = part(0)
    vc_ref[0] = part(1)
    _, ks = _head_norm_rope(part(2), g1_ref[...], cos, sin)
    _, kw = _head_norm_rope(part(4), g2_ref[...], cos, sin)
    vs = part(3)
    vw = part(5)
    ks_ref[0] = ks
    vs_ref[0] = vs
    kw_ref[0] = kw
    vw_ref[0] = vw
    ksb_ref[0] = ks.astype(BF16)
    vsb_ref[0] = vs.astype(BF16)
    kwb_ref[0] = kw.astype(BF16)
    vwb_ref[0] = vw.astype(BF16)


def _kv_proj(x, kv_gain, w_kvt, g1, g2, cos_t, sin_t, *, tm):
    bm, sm, d = x.shape
    half = HEAD_DIM // 2
    out_spec = pl.BlockSpec((1, KV_WIDTH, tm), lambda b, j: (b, 0, j))
    return pl.pallas_call(
        _kv_proj_kernel,
        grid=(bm, sm // tm),
        in_specs=[pl.BlockSpec((1, tm, d), lambda b, j: (b, j, 0)),
                  _const_spec(kv_gain.shape), _const_spec(w_kvt.shape), _const_spec(g1.shape),
                  _const_spec(g2.shape),
                  pl.BlockSpec((half, tm), lambda b, j: (0, j)),
                  pl.BlockSpec((half, tm), lambda b, j: (0, j))],
        out_specs=[out_spec] * 10,
        out_shape=[jax.ShapeDtypeStruct((bm, KV_WIDTH, sm), F32)] * 6
                  + [jax.ShapeDtypeStruct((bm, KV_WIDTH, sm), BF16)] * 4,
        compiler_params=_params(2),
        name="kv_projection",
    )(x, kv_gain, w_kvt, g1, g2, cos_t, sin_t)


def _q_proj_kernel(x_ref, sh_ref, sc_ref, ng_ref, w_ref, qg_ref, cos_ref, sin_ref, q_ref, qr_ref, gates_ref):
    h = (_rms(x_ref[0], ng_ref[...]) * (1.0 + sc_ref[0]) + sh_ref[0]).astype(BF16)
    pt = _dot_nt(w_ref[...], h)
    nq = N_HEADS * HEAD_DIM
    qn, qrot = _head_norm_rope(pt[:nq], qg_ref[...], cos_ref[...], sin_ref[...])
    scale = HEAD_DIM ** -0.5
    q_ref[0] = (qn * scale).T.astype(BF16)
    qr_ref[0] = (qrot * scale).T.astype(BF16)
    gates_ref[0] = jax.nn.sigmoid(pt[nq:]).T


def _q_proj(x, sh, sc, ng, w_int, q_gain, cos_t, sin_t, *, tm):
    bm, sm, d = x.shape
    nq = N_HEADS * HEAD_DIM
    half = HEAD_DIM // 2
    return pl.pallas_call(
        _q_proj_kernel,
        grid=(bm, sm // tm),
        in_specs=[pl.BlockSpec((1, tm, d), lambda b, j: (b, j, 0)),
                  _mod_spec(sh, tm), _mod_spec(sc, tm),
                  _const_spec(ng.shape), _const_spec(w_int.shape), _const_spec(q_gain.shape),
                  pl.BlockSpec((half, tm), lambda b, j: (0, j)),
                  pl.BlockSpec((half, tm), lambda b, j: (0, j))],
        out_specs=[pl.BlockSpec((1, tm, nq), lambda b, j: (b, j, 0)),
                   pl.BlockSpec((1, tm, nq), lambda b, j: (b, j, 0)),
                   pl.BlockSpec((1, tm, GATE_PAD), lambda b, j: (b, j, 0))],
        out_shape=[jax.ShapeDtypeStruct((bm, sm, nq), BF16),
                   jax.ShapeDtypeStruct((bm, sm, nq), BF16),
                   jax.ShapeDtypeStruct((bm, sm, GATE_PAD), F32)],
        compiler_params=_params(2),
        name="q_projection",
    )(x, sh, sc, ng, w_int, q_gain, cos_t, sin_t)


def _store_rows(xs_ref, xt):
    for c in range(xs_ref.shape[0]):
        xs_ref[c] = xt[c * LANES:(c + 1) * LANES, :].T


def _compress_halves(xs_ref, g, n, pe_a, pe_b, w1a_ref, w1b_ref):
    lo = (g % 2) * HEAD_DIM
    pieces = [xs_ref[g // 2, pl.ds(r, n, stride=CMP_STRIDE), :][:, lo:lo + HEAD_DIM]
              for r in range(CMP_STRIDE)]
    lhs = jnp.concatenate(pieces, axis=1)
    a = _dot((lhs + pe_a).astype(BF16), w1a_ref[...])
    b = _dot((lhs + pe_b).astype(BF16), w1b_ref[...])
    return a, b


def _compress_finish(a, b_next, b1, w2_ref, b2, gain):
    hid = jax.nn.gelu(a + b_next + b1)
    out = _dot(hid.astype(BF16), w2_ref[...]) + b2
    if gain is not None:
        ms = jnp.sum(out * out, axis=-1, keepdims=True) * (1.0 / HEAD_DIM)
        out = out * lax.rsqrt(ms + EPS) * gain
    return out


def _compress_kernel(x_ref, nxt_ref, pea_ref, peb_ref, w1a_ref, w1b_ref, b1_ref, w2_ref, b2_ref, gain_ref,
                     o_ref, xs_ref, xn_ref, *, norm):
    lc = x_ref.shape[-1]
    n = lc // CMP_STRIDE
    ln = nxt_ref.shape[-1]
    _store_rows(xs_ref, x_ref[0])
    _store_rows(xn_ref, nxt_ref[0])
    last_row = _iota((n, 1), 0) == n - 1
    gain = gain_ref[...] if norm else None
    for g in range(KV_HEADS):
        a, b = _compress_halves(xs_ref, g, n, pea_ref[...], peb_ref[...], w1a_ref, w1b_ref)
        _, bn = _compress_halves(xn_ref, g, ln // CMP_STRIDE, pea_ref[...], peb_ref[...], w1a_ref, w1b_ref)
        b_next = jnp.where(last_row, bn[0:1, :], pltpu.roll(b, shift=n - 1, axis=0))
        out = _compress_finish(a, b_next, b1_ref[...], w2_ref, b2_ref[...], gain)
        o_ref[0, g] = out.T[:HEAD_DIM]


def _compress_prompt(xt, cw, *, norm, lc=2048, ln=256):
    b, _, s = xt.shape
    lc = min(lc, s)
    n = lc // CMP_STRIDE
    n_next = s // ln
    return pl.pallas_call(
        functools.partial(_compress_kernel, norm=norm),
        grid=(b, s // lc),
        in_specs=[pl.BlockSpec((1, KV_WIDTH, lc), lambda i, c: (i, 0, c)),
                  pl.BlockSpec((1, KV_WIDTH, ln),
                               lambda i, c: (i, 0, jnp.minimum((c + 1) * (lc // ln), n_next - 1)))]
                 + [_const_spec(w.shape) for w in cw],
        out_specs=pl.BlockSpec((1, KV_HEADS, HEAD_DIM, n), lambda i, c: (i, 0, 0, c)),
        out_shape=jax.ShapeDtypeStruct((b, KV_HEADS, HEAD_DIM, s // CMP_STRIDE), F32),
        scratch_shapes=[pltpu.VMEM((KV_WIDTH // LANES, lc, LANES), F32),
                        pltpu.VMEM((KV_WIDTH // LANES, ln, LANES), F32)],
        compiler_params=_params(2),
        name="compress_prompt",
    )(xt, xt, *cw)


def _rank_body(sct_ref, sct, srow):
    def body(sp, rank):
        row = sct_ref[pl.ds(sp, 1), :]
        beats = jnp.where(row > sct, 1.0, jnp.where(row == sct, jnp.where(sp < srow, 1.0, 0.0), 0.0))
        return rank + beats
    return body


def _attn_prompt_kernel(q_ref, qr_ref, gates_ref, x_ref, g1_ref, kct_ref, vct_ref, ks_ref, vs_ref, kw_ref,
                        vw_ref, e_ref, ov_ref, wo_ref, o_ref, sct_ref, *, nc, ns, n_sel, tk):
    n = pl.program_id(1)
    qb = Q_BLOCK
    hq = HEADS_PER_GROUP * qb
    q0 = n * qb
    ncp = kct_ref.shape[-1]
    qpos = q0 + _iota((qb, 1), 0)
    cur = qpos // SLC_BLOCK
    gates = gates_ref[0]
    q_all = q_ref[0]
    qr_all = qr_ref[0]

    ci = _iota((qb, ncp), 1)
    vis = (ci < nc) & (ci * CMP_STRIDE + (CMP_BLOCK - 1) <= qpos)
    vis_bias = jnp.where(vis, 0.0, NEG_INF)
    vis_one = jnp.where(vis, 1.0, 0.0)
    blk = _iota((qb, LANES), 1)
    valid = blk <= cur
    forced = (blk == 0) | (blk == cur) | (blk == cur - 1)
    srow = _iota((LANES, qb), 0)
    n_rank = jnp.minimum(ns, 2 * n + 2)
    n_tiles = (q0 + qb + tk - 1) // tk

    proj = jnp.zeros((qb, wo_ref.shape[-1]), F32)
    for g in range(KV_HEADS):
        heads = range(g * HEADS_PER_GROUP, (g + 1) * HEADS_PER_GROUP)
        hs = slice(g * HEAD_DIM, (g + 1) * HEAD_DIM)
        qg = jnp.concatenate([q_all[:, h * HEAD_DIM:(h + 1) * HEAD_DIM] for h in heads], axis=0)
        qrg = jnp.concatenate([qr_all[:, h * HEAD_DIM:(h + 1) * HEAD_DIM] for h in heads], axis=0)

        s = _dot(qg, kct_ref[0, g].astype(BF16)).reshape(HEADS_PER_GROUP, qb, ncp) + vis_bias[None]
        m = jnp.max(s, axis=-1, keepdims=True)
        p = jnp.exp(s - m) * vis_one[None]
        pc = p / jnp.maximum(jnp.sum(p, axis=-1, keepdims=True), TINY)
        oc = _dot_nt(pc.reshape(hq, ncp).astype(BF16), vct_ref[0, g].astype(BF16))

        imp = _split_dot(pc[0] + pc[1] + pc[2] + pc[3], ov_ref[...])
        score = jnp.where(valid, jnp.where(forced, FORCE_SCORE, imp), NEG_INF)
        sct = score.T
        sct_ref[...] = sct
        rank = lax.fori_loop(0, n_rank, _rank_body(sct_ref, sct, srow), jnp.zeros((LANES, qb), F32))
        sel = jnp.where(valid, jnp.where(rank < n_sel, 1.0, 0.0).T, 0.0).astype(BF16)

        def tile_body(j, carry, qrg=qrg, sel=sel, hs=hs):
            m_i, l_i, acc = carry
            k0 = pl.multiple_of(j * tk, tk)
            kt = ks_ref[0, hs, pl.ds(k0, tk)]
            vt = vs_ref[0, hs, pl.ds(k0, tk)]
            st = _dot(qrg, kt).reshape(HEADS_PER_GROUP, qb, tk)
            chosen = _dot(sel, e_ref[:, pl.ds(k0, tk)])
            kpos = k0 + _iota((qb, tk), 1)
            bias = jnp.where(kpos <= qpos, jnp.where(chosen > 0.5, 0.0, NEG_INF), NEG_INF)
            st = (st + bias[None]).reshape(hq, tk)
            m_new = jnp.maximum(m_i, jnp.max(st, axis=-1, keepdims=True))
            alpha = jnp.exp(m_i - m_new)
            pt = jnp.exp(st - m_new)
            l_new = alpha * l_i + jnp.sum(pt, axis=-1, keepdims=True)
            acc = alpha * acc + _dot_nt(pt.astype(BF16), vt)
            return m_new, l_new, acc

        m_i, l_i, acc = lax.fori_loop(
            0, n_tiles, tile_body,
            (jnp.full((hq, 1), NEG_INF, F32), jnp.zeros((hq, 1), F32), jnp.zeros((hq, HEAD_DIM), F32)))
        os_ = acc / jnp.maximum(l_i, TINY)

        kts, vts, biases = [], [], []
        for t in range(WINDOW // qb + 1):
            off = q0 - WINDOW + qb * t
            offc = pl.multiple_of(jnp.maximum(off, 0), qb)
            kts.append(kw_ref[0, hs, pl.ds(offc, qb)])
            vts.append(vw_ref[0, hs, pl.ds(offc, qb)])
            wpos = off + _iota((qb, qb), 1)
            dpos = qpos - wpos
            ok = (dpos >= 0) & (dpos <= WINDOW) & (wpos >= 0)
            biases.append(jnp.where(ok, 0.0, NEG_INF))
        kwt = jnp.concatenate(kts, axis=1)
        vwt = jnp.concatenate(vts, axis=1)
        wk = kwt.shape[1]
        sw = _dot(qrg, kwt).reshape(HEADS_PER_GROUP, qb, wk) + jnp.concatenate(biases, axis=1)[None]
        sw = sw.reshape(hq, wk)
        pw = jnp.exp(sw - jnp.max(sw, axis=-1, keepdims=True))
        ow = _dot_nt(pw.astype(BF16), vwt) / jnp.maximum(jnp.sum(pw, axis=-1, keepdims=True), TINY)

        for i, h in enumerate(heads):
            rs = slice(i * qb, (i + 1) * qb)
            o_h = (oc[rs] * gates[:, 3 * h:3 * h + 1] + os_[rs] * gates[:, 3 * h + 1:3 * h + 2]
                   + ow[rs] * gates[:, 3 * h + 2:3 * h + 3])
            proj = proj + _dot(o_h.astype(BF16), wo_ref[h])
    o_ref[0] = x_ref[0] + g1_ref[0] * proj


def _attn_prompt(q, qr, gates, x, gate1, kct, vct, ksb, vsb, kwb, vwb, e_mat, ov, wo, *, tk=512):
    b, s, d = x.shape
    nq = q.shape[-1]
    ncp = kct.shape[-1]
    nc = s // CMP_STRIDE - 1
    ns = -(-s // SLC_BLOCK)
    tk = min(tk, s)
    kern = functools.partial(_attn_prompt_kernel, nc=nc, ns=ns, n_sel=min(N_SELECT, ns), tk=tk)
    full_kv = pl.BlockSpec((1, KV_WIDTH, s), lambda i, j: (i, 0, 0), pipeline_mode=pl.Buffered(1))
    cmp_spec = pl.BlockSpec((1, KV_HEADS, HEAD_DIM, ncp), lambda i, j: (i, 0, 0, 0))
    return pl.pallas_call(
        kern,
        grid=(b, s // Q_BLOCK),
        in_specs=[pl.BlockSpec((1, Q_BLOCK, nq), lambda i, j: (i, j, 0)),
                  pl.BlockSpec((1, Q_BLOCK, nq), lambda i, j: (i, j, 0)),
                  pl.BlockSpec((1, Q_BLOCK, GATE_PAD), lambda i, j: (i, j, 0)),
                  pl.BlockSpec((1, Q_BLOCK, d), lambda i, j: (i, j, 0)),
                  pl.BlockSpec((1, 1, d), lambda i, j: (i, 0, 0)),
                  cmp_spec, cmp_spec, full_kv, full_kv, full_kv, full_kv,
                  pl.BlockSpec(e_mat.shape, lambda i, j: (0, 0), pipeline_mode=pl.Buffered(1)),
                  _const_spec(ov.shape), _const_spec(wo.shape)],
        out_specs=pl.BlockSpec((1, Q_BLOCK, d), lambda i, j: (i, j, 0)),
        out_shape=jax.ShapeDtypeStruct((b, s, d), F32),
        scratch_shapes=[pltpu.VMEM((LANES, Q_BLOCK), F32)],
        compiler_params=_params(2),
        name="nsa_prompt",
    )(q, qr, gates, x, gate1, kct, vct, ksb, vsb, kwb, vwb, e_mat, ov, wo)


def _attn_sample_kernel(pt_ref, ck_ref, cv_ref, sk_ref, sv_ref, wk_ref, wv_ref, q_ref, qr_ref, gates_ref,
                        ksn_ref, vsn_ref, kwn_ref, vwn_ref, kwc_ref, vwc_ref,
                        kpea_ref, kpeb_ref, kw1a_ref, kw1b_ref, kb1_ref, kw2_ref, kb2_ref, kgain_ref,
                        vpea_ref, vpeb_ref, vw1a_ref, vw1b_ref, vb1_ref, vw2_ref, vb2_ref, vgain_ref,
                        e_ref, ov_ref,
                        o_ref, wko_ref, wvo_ref,
                        sl_ck, sl_cv, sl_sk, sl_sv, xs_ref, *, past, page, n_sel):
    del pt_ref, vgain_ref
    p = pl.program_id(1)
    col = pl.multiple_of(p * page, page)
    sl_ck[:, pl.ds(col, page)] = ck_ref[0]
    sl_cv[:, pl.ds(col, page)] = cv_ref[0]
    sl_sk[:, pl.ds(col, page)] = sk_ref[0]
    sl_sv[:, pl.ds(col, page)] = sv_ref[0]

    @pl.when(p == pl.num_programs(1) - 1)
    def _():
        nh = N_HEADS
        qpos = past
        n = past // CMP_STRIDE
        nc = n - 1
        cur = qpos // SLC_BLOCK
        win = wk_ref.shape[-1]
        hrow = _iota((nh, 1), 0) // HEADS_PER_GROUP

        def by_group(fn, width):
            out = jnp.zeros((nh, width), F32)
            for g in range(KV_HEADS):
                out = jnp.where(hrow == g, fn(g), out)
            return out

        def compress(slab, pea, peb, w1a, w1b, b1, w2, b2, gain):
            _store_rows(xs_ref, slab[...])
            outs = []
            for g in range(KV_HEADS):
                a, b = _compress_halves(xs_ref, g, n, pea[...], peb[...], w1a, w1b)
                outs.append(_compress_finish(a, pltpu.roll(b, shift=n - 1, axis=0), b1[...], w2, b2[...], gain))
            return outs

        kc = compress(sl_ck, kpea_ref, kpeb_ref, kw1a_ref, kw1b_ref, kb1_ref, kw2_ref, kb2_ref, kgain_ref[...])
        vc = compress(sl_cv, vpea_ref, vpeb_ref, vw1a_ref, vw1b_ref, vb1_ref, vw2_ref, vb2_ref, None)

        q = q_ref[0]
        qr = qr_ref[0]
        qr_lo = qr[:, :HEAD_DIM]
        qr_f = qr_lo.astype(F32)

        sc = by_group(lambda g: _dot_nt(q, kc[g].astype(BF16)), n)
        ci = _iota((nh, n), 1)
        vis = (ci < nc) & (ci * CMP_STRIDE + (CMP_BLOCK - 1) <= qpos)
        sc = jnp.where(vis, sc, NEG_INF)
        pc = jnp.where(vis, jnp.exp(sc - jnp.max(sc, axis=-1, keepdims=True)), 0.0)
        pc = pc / jnp.maximum(jnp.sum(pc, axis=-1, keepdims=True), TINY)
        o_c = by_group(lambda g: _dot(pc.astype(BF16), vc[g].astype(BF16)), LANES)[:, :HEAD_DIM]

        grow = _iota((nh, 1), 0)
        psum = jnp.zeros((nh, n), F32)
        for g in range(KV_HEADS):
            pg = jnp.sum(pc[g * HEADS_PER_GROUP:(g + 1) * HEADS_PER_GROUP], axis=0, keepdims=True)
            psum = jnp.where(grow == g, pg, psum)
        imp = _split_dot(psum, ov_ref[...])
        blk = _iota((nh, LANES), 1)
        valid = blk <= cur
        forced = (blk == 0) | (blk == cur) | (blk == cur - 1)
        score = jnp.where(valid, jnp.where(forced, FORCE_SCORE, imp), NEG_INF)
        sct = jnp.concatenate([score, jnp.zeros((LANES - nh, LANES), F32)], axis=0).T
        srow = _iota((LANES, LANES), 0)
        rank = jnp.zeros((LANES, LANES), F32)
        for sp in range(cur + 1):
            row = sct[sp:sp + 1, :]
            rank = rank + jnp.where(row > sct, 1.0, jnp.where(row == sct, jnp.where(sp < srow, 1.0, 0.0), 0.0))
        sel_g = jnp.where(valid, jnp.where(rank < n_sel, 1.0, 0.0).T[:nh], 0.0)
        sel = by_group(lambda g: sel_g[g:g + 1, :], LANES)

        chosen = _dot(sel.astype(BF16), e_ref[...])
        kpos = _iota((nh, past), 1)
        bias = jnp.where(kpos <= qpos, jnp.where(chosen > 0.5, 0.0, NEG_INF), NEG_INF)
        ss = by_group(lambda g: _dot(qr_lo, sl_sk[g * HEAD_DIM:(g + 1) * HEAD_DIM, :].astype(BF16)), past) + bias
        s_new = jnp.sum(qr_f * ksn_ref[0], axis=-1, keepdims=True)
        s_new = jnp.where(sel[:, cur:cur + 1] > 0.5, s_new, NEG_INF)
        m = jnp.maximum(jnp.max(ss, axis=-1, keepdims=True), s_new)
        ps = jnp.exp(ss - m)
        p_new = jnp.exp(s_new - m)
        den = jnp.maximum(jnp.sum(ps, axis=-1, keepdims=True) + p_new, TINY)
        psb = ps.astype(BF16)
        o_s = by_group(lambda g: _dot_nt(psb, sl_sv[g * HEAD_DIM:(g + 1) * HEAD_DIM, :].astype(BF16)), HEAD_DIM)
        o_s = (o_s + p_new * vsn_ref[0]) / den

        wpos = qpos - win + _iota((nh, win), 1)
        dpos = qpos - wpos
        okw = (dpos >= 0) & (dpos <= WINDOW) & (wpos >= 0)
        sw = by_group(lambda g: _dot(qr_lo, wk_ref[0, g * HEAD_DIM:(g + 1) * HEAD_DIM, :].astype(BF16)), win)
        sw = sw + jnp.where(okw, 0.0, NEG_INF)
        sw_new = jnp.sum(qr_f * kwn_ref[0], axis=-1, keepdims=True)
        mw = jnp.maximum(jnp.max(sw, axis=-1, keepdims=True), sw_new)
        pw = jnp.exp(sw - mw)
        pw_new = jnp.exp(sw_new - mw)
        denw = jnp.maximum(jnp.sum(pw, axis=-1, keepdims=True) + pw_new, TINY)
        pwb = pw.astype(BF16)
        o_w = by_group(lambda g: _dot_nt(pwb, wv_ref[0, g * HEAD_DIM:(g + 1) * HEAD_DIM, :].astype(BF16)), HEAD_DIM)
        o_w = (o_w + pw_new * vwn_ref[0]) / denw

        gates = gates_ref[0]
        o_ref[0] = o_c * gates[:, 0:1] + o_s * gates[:, 1:2] + o_w * gates[:, 2:3]

        lane = _iota((KV_WIDTH, win), 1)
        wko_ref[0] = jnp.where(lane == win - 1, kwc_ref[0], pltpu.roll(wk_ref[0], shift=win - 1, axis=1))
        wvo_ref[0] = jnp.where(lane == win - 1, vwc_ref[0], pltpu.roll(wv_ref[0], shift=win - 1, axis=1))


def _attn_sample(page_table, caches, win_k, win_v, q, qr, gates, new16, new_cols, cw_k, cw_v, e_mat, ov):
    bd, n_pages = page_table.shape
    page = caches[0].shape[-1]
    past = n_pages * page
    win = win_k.shape[-1]
    cur = past // SLC_BLOCK
    kern = functools.partial(_attn_sample_kernel, past=past, page=page, n_sel=min(N_SELECT, cur + 1))

    def per_seq(a):
        nd = a.ndim
        return pl.BlockSpec((1,) + a.shape[1:], lambda b, p, pt: (b,) + (0,) * (nd - 1))

    def const(a):
        nd = a.ndim
        return pl.BlockSpec(a.shape, lambda b, p, pt: (0,) * nd)

    page_spec = pl.BlockSpec((1, KV_WIDTH, page), lambda b, p, pt: (pt[b, p], 0, 0))
    consts = list(cw_k) + list(cw_v) + [e_mat, ov]
    grid_spec = pltpu.PrefetchScalarGridSpec(
        num_scalar_prefetch=1,
        grid=(bd, n_pages),
        in_specs=[page_spec] * 4 + [per_seq(win_k), per_seq(win_v), per_seq(q), per_seq(qr), per_seq(gates)]
                 + [per_seq(a) for a in new16] + [per_seq(a) for a in new_cols] + [const(a) for a in consts],
        out_specs=[pl.BlockSpec((1, N_HEADS, HEAD_DIM), lambda b, p, pt: (b, 0, 0)),
                   pl.BlockSpec((1, KV_WIDTH, win), lambda b, p, pt: (b, 0, 0)),
                   pl.BlockSpec((1, KV_WIDTH, win), lambda b, p, pt: (b, 0, 0))],
        scratch_shapes=[pltpu.VMEM((KV_WIDTH, past), F32)] * 4
                       + [pltpu.VMEM((KV_WIDTH // LANES, past, LANES), F32)],
    )
    return pl.pallas_call(
        kern,
        grid_spec=grid_spec,
        out_shape=[jax.ShapeDtypeStruct((bd, N_HEADS, HEAD_DIM), F32),
                   jax.ShapeDtypeStruct((bd, KV_WIDTH, win), F32),
                   jax.ShapeDtypeStruct((bd, KV_WIDTH, win), F32)],
        compiler_params=_params(2),
        name="nsa_decode",
    )(page_table, *caches, win_k, win_v, q, qr, gates, *new16, *new_cols, *consts)


def _out_proj_kernel(o_ref, x_ref, g_ref, w_ref, y_ref):
    y_ref[0] = x_ref[0] + g_ref[0] * _dot(o_ref[0].astype(BF16), w_ref[...])


def _out_proj(o, x, gate, w):
    bm, sm, d = x.shape
    spec = pl.BlockSpec((1, sm, d), lambda b: (b, 0, 0))
    return pl.pallas_call(
        _out_proj_kernel,
        grid=(bm,),
        in_specs=[pl.BlockSpec((1, sm, o.shape[-1]), lambda b: (b, 0, 0)), spec, spec, _const_spec(w.shape)],
        out_specs=spec,
        out_shape=jax.ShapeDtypeStruct((bm, sm, d), F32),
        compiler_params=_params(1),
        name="attn_out_proj",
    )(o, x, gate, w)


def _rope_tables(pos):
    half = HEAD_DIM // 2
    inv_freq = ROPE_THETA ** (-jnp.arange(half, dtype=F32) / half)
    ang = inv_freq[:, None] * pos.astype(F32)[None, :]
    return jnp.cos(ang), jnp.sin(ang)


def _overlap_matrix(n_rows, nc):
    i = np.arange(n_rows)[:, None] * CMP_STRIDE
    s = np.arange(LANES)[None, :] * SLC_BLOCK
    ov = (i < s + SLC_BLOCK) & (i + CMP_BLOCK > s) & (np.arange(n_rows)[:, None] < nc)
    return jnp.asarray(ov.astype(np.float32), dtype=BF16)


def _expand_matrix(n_keys):
    e = (np.arange(n_keys)[None, :] // SLC_BLOCK) == np.arange(LANES)[:, None]
    return jnp.asarray(e.astype(np.float32), dtype=BF16)


def _compress_weights(t, cmp_pe, cmp_w1, cmp_b1, cmp_w2, cmp_b2, gain):
    half = CMP_STRIDE * HEAD_DIM
    pad = LANES - HEAD_DIM
    return (cmp_pe[t, :CMP_STRIDE].reshape(1, half), cmp_pe[t, CMP_STRIDE:].reshape(1, half),
            cmp_w1[t, :half].astype(BF16), cmp_w1[t, half:].astype(BF16), cmp_b1[t][None, :],
            jnp.pad(cmp_w2[t], ((0, 0), (0, pad))).astype(BF16), jnp.pad(cmp_b2[t], (0, pad))[None, :],
            jnp.pad(gain, (0, pad))[None, :])


def _to_rows(t):
    b, _, n = t.shape
    return t.reshape(b, KV_HEADS, HEAD_DIM, n).transpose(0, 3, 1, 2)


def kernel(x_prompt, x_sample, cache_cmp_k, cache_cmp_v, cache_slc_k, cache_slc_v, state_win_k, state_win_v,
           page_table, c_prompt, c_sample, w_ada, b_ada, norm_g, a_w_in, a_v_gain, a_w_s, a_b_s, a_w_out,
           b_w_in, b_q_gain, b_w_out, kv_gain, w_kv, k_gains, cmp_pe, cmp_w1, cmp_b1, cmp_w2, cmp_b2,
           ffn_w_gu, ffn_w_down):
    b, s, d = x_prompt.shape
    bd = x_sample.shape[0]
    assert x_sample.shape[1] == 1 and w_ada.shape[0] == 2 and a_w_in.shape[0] == 1 and b_w_in.shape[0] == 1
    n_phys, page = cache_cmp_k.shape[:2]
    past = page_table.shape[1] * page
    win = state_win_k.shape[1]
    assert win == WINDOW and s % 512 == 0 and s >= WINDOW and past % SLC_BLOCK == 0
    width = a_w_out.shape[1]
    gd = width // A_GROUPS

    rows = b + bd
    rows_p = -(-rows // 8) * 8
    c_all = jnp.pad(jnp.concatenate([c_prompt, c_sample], axis=0), ((0, rows_p - rows), (0, 0)))
    mod = _ada(c_all, w_ada, b_ada)

    def mods(layer, lo, hi, per_token):
        m = mod[layer, lo:hi].reshape(hi - lo, 6, d)
        return [m[None, :, k, :] if per_token else m[:, None, k, :] for k in range(6)]

    w_in_a = a_w_in[0].astype(BF16)
    w_out_a = a_w_out[0].astype(BF16)
    vg = a_v_gain[0][None, :]
    bias_p = jnp.repeat(a_b_s[0].T, gd, axis=1)
    diag_s = jnp.repeat(a_w_s[0, :, 0, 0], gd)[None, :]
    bias_s = jnp.repeat(a_b_s[0, :, 0], gd)[None, :]
    w_gu = ffn_w_gu.astype(BF16)
    w_down = ffn_w_down.astype(BF16)
    w_kvt = w_kv.T.astype(BF16)
    w_qt = jnp.pad(b_w_in[0].T, ((0, GATE_PAD - 3 * N_HEADS), (0, 0))).astype(BF16)
    w_o = b_w_out[0].astype(BF16)
    kg1 = k_gains[1][:, None]
    kg2 = k_gains[2][:, None]
    qg = b_q_gain[0][:, None]
    cw_k = _compress_weights(0, cmp_pe, cmp_w1, cmp_b1, cmp_w2, cmp_b2, k_gains[0])
    cw_v = _compress_weights(1, cmp_pe, cmp_w1, cmp_b1, cmp_w2, cmp_b2, k_gains[0])
    ng = norm_g[:, :, None, :]

    tm = 512
    sh1, sc1, gt1, sh2, sc2, gt2 = mods(0, 0, b, False)
    x1, v_p = _mixer_a(x_prompt, sh1, sc1, gt1, ng[0, 0], w_in_a, vg, a_w_s[0], bias_p, w_out_a,
                       single_token=False, tm=tm)
    x2 = _ffn(x1, sh2, sc2, gt2, ng[0, 1], w_gu[0], w_down[0], tm=tm)

    cos_p, sin_p = _rope_tables(jnp.arange(s, dtype=jnp.int32))
    kct_p, vct_p, kst_p, vst_p, kwt_p, vwt_p, ksb, vsb, kwb, vwb = _kv_proj(
        x2, kv_gain[None, :], w_kvt, kg1, kg2, cos_p, sin_p, tm=tm)
    kcc = _compress_prompt(kct_p, cw_k, norm=True)
    vcc = _compress_prompt(vct_p, cw_v, norm=False)

    sh1, sc1, gt1, sh2, sc2, gt2 = mods(1, 0, b, False)
    q_p, qr_p, gates_p = _q_proj(x2, sh1, sc1, ng[1, 0], w_qt, qg, cos_p, sin_p, tm=tm)
    ncp = s // CMP_STRIDE
    x3 = _attn_prompt(q_p, qr_p, gates_p, x2, gt1, kcc, vcc, ksb, vsb, kwb, vwb,
                      _expand_matrix(s), _overlap_matrix(ncp, ncp - 1),
                      w_o.reshape(N_HEADS, HEAD_DIM, d))
    y_prompt = _ffn(x3, sh2, sc2, gt2, ng[1, 1], w_gu[1], w_down[1], tm=tm)

    xs = x_sample.reshape(1, bd, d)
    sh1, sc1, gt1, sh2, sc2, gt2 = mods(0, b, b + bd, True)
    xs1, v_s = _mixer_a(xs, sh1, sc1, gt1, ng[0, 0], w_in_a, vg, diag_s, bias_s, w_out_a,
                        single_token=True, tm=bd)
    xs2 = _ffn(xs1, sh2, sc2, gt2, ng[0, 1], w_gu[0], w_down[0], tm=bd)

    cos_s, sin_s = _rope_tables(jnp.full((bd,), past, dtype=jnp.int32))
    new_t = _kv_proj(xs2, kv_gain[None, :], w_kvt, kg1, kg2, cos_s, sin_s, tm=bd)[:6]
    sh1, sc1, gt1, sh2, sc2, gt2 = mods(1, b, b + bd, True)
    q_s, qr_s, gates_s = _q_proj(xs2, sh1, sc1, ng[1, 0], w_qt, qg, cos_s, sin_s, tm=bd)

    def per_head(t):
        r = t[0].T.reshape(bd, KV_HEADS, 1, HEAD_DIM)
        return jnp.broadcast_to(r, (bd, KV_HEADS, HEADS_PER_GROUP, HEAD_DIM)).reshape(bd, N_HEADS, HEAD_DIM)

    def lane_pad(t):
        return jnp.pad(t.reshape(bd, N_HEADS, HEAD_DIM), ((0, 0), (0, 0), (0, LANES - HEAD_DIM)))

    caches = [c.transpose(0, 2, 3, 1).reshape(n_phys, KV_WIDTH, page)
              for c in (cache_cmp_k, cache_cmp_v, cache_slc_k, cache_slc_v)]
    win_k = state_win_k.transpose(0, 2, 3, 1).reshape(bd, KV_WIDTH, win)
    win_v = state_win_v.transpose(0, 2, 3, 1).reshape(bd, KV_WIDTH, win)
    new16 = [per_head(new_t[i]) for i in (2, 3, 4, 5)]
    new_cols = [new_t[i][0].T[:, :, None] for i in (4, 5)]
    gates16 = gates_s[0, :, :3 * N_HEADS].reshape(bd, N_HEADS, 3)
    o_s, wk_new, wv_new = _attn_sample(
        page_table, caches, win_k, win_v, lane_pad(q_s), lane_pad(qr_s), gates16, new16, new_cols,
        cw_k, cw_v, _expand_matrix(past), _overlap_matrix(past // CMP_STRIDE, past // CMP_STRIDE - 1))
    xs3 = _out_proj(o_s.reshape(1, bd, N_HEADS * HEAD_DIM), xs2, gt1, w_o)
    y_sample = _ffn(xs3, sh2, sc2, gt2, ng[1, 1], w_gu[1], w_down[1], tm=bd)

    keep = min(WINDOW, s)
    new_rows = [t[0].T.reshape(bd, 1, KV_HEADS, HEAD_DIM) for t in new_t[:4]]
    return (y_prompt, y_sample.reshape(bd, 1, d), v_p[None], v_s.reshape(1, bd, 1, width),
            _to_rows(kct_p), _to_rows(vct_p), _to_rows(kst_p), _to_rows(vst_p),
            _to_rows(kwt_p[:, :, s - keep:]), _to_rows(vwt_p[:, :, s - keep:]),
            new_rows[0], new_rows[1], new_rows[2], new_rows[3],
            _to_rows(wk_new), _to_rows(wv_new))
```

```python
import functools

import numpy as np
import jax
import jax.numpy as jnp
from jax import lax
from jax.experimental import pallas as pl
from jax.experimental.pallas import tpu as pltpu

F32 = jnp.float32
BF16 = jnp.bfloat16

EPS = 1e-6
NEG_INF = -1e30
FORCE_SCORE = 1e9
TINY = 1e-30
ROPE_THETA = 10000.0
LOG2E = 1.4426950408889634
KNOCKED_OUT = -3e38

N_HEADS = 16
HEAD_DIM = 64
KV_HEADS = 4
HEADS_PER_GROUP = N_HEADS // KV_HEADS
KV_WIDTH = KV_HEADS * HEAD_DIM
V_AUG = HEAD_DIM + 16
CHUNK = 128
A_GROUPS = 8
CMP_STRIDE = 16
CMP_BLOCK = 2 * CMP_STRIDE
SLC_BLOCK = 64
N_SELECT = 16
WINDOW = 512
Q_BLOCK = 128
LANES = 128
GATE_PAD = 128
VMEM_LIMIT = 56 * 1024 * 1024

_NT = (((1,), (1,)), ((), ()))


def _params(n_grid):
    return pltpu.CompilerParams(dimension_semantics=("arbitrary",) * n_grid, vmem_limit_bytes=VMEM_LIMIT)


def _dot(a, b):
    return jnp.dot(a, b, preferred_element_type=F32)


def _dot_nt(a, b):
    return lax.dot_general(a, b, _NT, preferred_element_type=F32)


def _rms(x, g):
    return x * lax.rsqrt(jnp.mean(x * x, axis=-1, keepdims=True) + EPS) * g


def _split_dot(x, w):
    hi = x.astype(BF16)
    lo = (x - hi.astype(F32)).astype(BF16)
    return _dot(hi, w) + _dot(lo, w)


def _iota(shape, dim):
    return lax.broadcasted_iota(jnp.int32, shape, dim)


def _const_spec(shape):
    nd = len(shape)
    return pl.BlockSpec(shape, lambda *_: (0,) * nd)


def _mod_spec(m, tm):
    d = m.shape[-1]
    if m.shape[1] == 1:
        return pl.BlockSpec((1, 1, d), lambda b, j: (b, 0, 0))
    return pl.BlockSpec((1, tm, d), lambda b, j: (b, j, 0))


def _ada_kernel(c_ref, w_ref, b_ref, o_ref):
    c = c_ref[...]
    s = (c * jax.nn.sigmoid(c)).astype(BF16)
    o_ref[0] = _dot(s, w_ref[0].astype(BF16)) + b_ref[0]


def _ada(c_all, w_ada, b_ada, tn=1536):
    n_layers, d, d6 = w_ada.shape
    m = c_all.shape[0]
    return pl.pallas_call(
        _ada_kernel,
        grid=(n_layers, d6 // tn),
        in_specs=[pl.BlockSpec((m, d), lambda l, j: (0, 0)),
                  pl.BlockSpec((1, d, tn), lambda l, j: (l, 0, j)),
                  pl.BlockSpec((1, 1, tn), lambda l, j: (l, 0, j))],
        out_specs=pl.BlockSpec((1, m, tn), lambda l, j: (l, 0, j)),
        out_shape=jax.ShapeDtypeStruct((n_layers, m, d6), F32),
        compiler_params=_params(2),
        name="ada_modulation",
    )(c_all, w_ada, b_ada.reshape(n_layers, 1, d6))


def _mixer_a_kernel(x_ref, sh_ref, sc_ref, gt_ref, ng_ref, win_ref, vg_ref, ws_ref, bias_ref, wout_ref,
                    x1_ref, v_ref, um_ref, *, single_token):
    x = x_ref[0]
    h = _rms(x, ng_ref[...]) * (1.0 + sc_ref[0]) + sh_ref[0]
    act = jax.nn.gelu(_dot(h.astype(BF16), win_ref[...]))
    width = act.shape[1] // 2
    u = act[:, :width]
    v = _rms(act[:, width:], vg_ref[...])
    if single_token:
        v_ref[0] = v
        um_ref[...] = (u * (v * ws_ref[...] + bias_ref[...])).astype(BF16)
    else:
        tm = x.shape[0]
        gd = width // A_GROUPS

        @pl.when(pl.program_id(1) == pl.num_programs(1) - 1)
        def _():
            v_ref[0] = v[tm - CHUNK:, :]

        causal = _iota((CHUNK, CHUNK), 0) >= _iota((CHUNK, CHUNK), 1)
        for g in range(A_GROUPS):
            wg = jnp.where(causal, ws_ref[g], 0.0).astype(BF16)
            cs = slice(g * gd, (g + 1) * gd)
            for c in range(tm // CHUNK):
                rs = slice(c * CHUNK, (c + 1) * CHUNK)
                mixed = _dot(wg, v[rs, cs].astype(BF16)) + bias_ref[:, cs]
                um_ref[rs, cs] = (u[rs, cs] * mixed).astype(BF16)
    x1_ref[0] = x + gt_ref[0] * _dot(um_ref[...], wout_ref[...])


def _mixer_a(x, sh, sc, gt, ng, w_in, v_gain, ws, bias, w_out, *, single_token, tm):
    bm, sm, d = x.shape
    width = w_out.shape[0]
    v_rows = sm if single_token else CHUNK
    kern = functools.partial(_mixer_a_kernel, single_token=single_token)
    return pl.pallas_call(
        kern,
        grid=(bm, sm // tm),
        in_specs=[pl.BlockSpec((1, tm, d), lambda b, j: (b, j, 0)),
                  _mod_spec(sh, tm), _mod_spec(sc, tm), _mod_spec(gt, tm),
                  _const_spec(ng.shape), _const_spec(w_in.shape), _const_spec(v_gain.shape),
                  _const_spec(ws.shape), _const_spec(bias.shape), _const_spec(w_out.shape)],
        out_specs=[pl.BlockSpec((1, tm, d), lambda b, j: (b, j, 0)),
                   pl.BlockSpec((1, v_rows, width), lambda b, j: (b, 0, 0))],
        out_shape=[jax.ShapeDtypeStruct((bm, sm, d), F32),
                   jax.ShapeDtypeStruct((bm, v_rows, width), F32)],
        scratch_shapes=[pltpu.VMEM((tm, width), BF16)],
        compiler_params=_params(2),
        name="mixer_a",
    )(x, sh, sc, gt, ng, w_in, v_gain, ws, bias, w_out)


def _ffn_kernel(x_ref, sh_ref, sc_ref, gt_ref, ng_ref, wgu_ref, wd_ref, o_ref, act_ref, *, fc):
    x = x_ref[0]
    h = (_rms(x, ng_ref[...]) * (1.0 + sc_ref[0]) + sh_ref[0]).astype(BF16)
    dff = wd_ref.shape[0]
    for j in range(dff // fc):
        g = _dot(h, wgu_ref[:, j * fc:(j + 1) * fc])
        u = _dot(h, wgu_ref[:, dff + j * fc:dff + (j + 1) * fc])
        act_ref[:, j * fc:(j + 1) * fc] = (g * jax.nn.sigmoid(g) * u).astype(BF16)
    o_ref[0] = x + gt_ref[0] * _dot(act_ref[...], wd_ref[...])


def _ffn(x, sh, sc, gt, ng, w_gu, w_down, *, tm, fc=256):
    bm, sm, d = x.shape
    dff = w_down.shape[0]
    assert dff % fc == 0
    return pl.pallas_call(
        functools.partial(_ffn_kernel, fc=fc),
        grid=(bm, sm // tm),
        in_specs=[pl.BlockSpec((1, tm, d), lambda b, j: (b, j, 0)),
                  _mod_spec(sh, tm), _mod_spec(sc, tm), _mod_spec(gt, tm),
                  _const_spec(ng.shape), _const_spec(w_gu.shape), _const_spec(w_down.shape)],
        out_specs=pl.BlockSpec((1, tm, d), lambda b, j: (b, j, 0)),
        out_shape=jax.ShapeDtypeStruct((bm, sm, d), F32),
        scratch_shapes=[pltpu.VMEM((tm, dff), BF16)],
        compiler_params=_params(2),
        name="swiglu_ffn",
    )(x, sh, sc, gt, ng, w_gu, w_down)


def _head_norm_rope(t, gain_col, cos, sin):
    rows, n = t.shape
    t3 = t.reshape(rows // HEAD_DIM, HEAD_DIM, n)
    y = t3 * lax.rsqrt(jnp.mean(t3 * t3, axis=1, keepdims=True) + EPS) * gain_col[None]
    half = HEAD_DIM // 2
    y1 = y[:, :half, :]
    y2 = y[:, half:, :]
    c = cos[None]
    s = sin[None]
    rot = jnp.concatenate([y1 * c - y2 * s, y2 * c + y1 * s], axis=1)
    return y.reshape(rows, n), rot.reshape(rows, n)


def _kv_proj_kernel(x_ref, kvg_ref, w_ref, g1_ref, g2_ref, cos_ref, sin_ref,
                    kc_ref, vc_ref, ks_ref, vs_ref, kw_ref, vw_ref, ksb_ref, vsb_ref, kwb_ref, vwb_ref):
    xn = _rms(x_ref[0], kvg_ref[...]).astype(BF16)
    kvt = _dot_nt(w_ref[...], xn)
    cos = cos_ref[...]
    sin = sin_ref[...]

    def part(i):
        return kvt[i * KV_WIDTH:(i + 1) * KV_WIDTH]

    kc_ref[0] = part(0)
    vc_ref[0] = part(1)
    _, ks = _head_norm_rope(part(2), g1_ref[...], cos, sin)
    _, kw = _head_norm_rope(part(4), g2_ref[...], cos, sin)
    vs = part(3)
    vw = part(5)
    ks_ref[0] = ks
    vs_ref[0] = vs
    kw_ref[0] = kw
    vw_ref[0] = vw
    ksb_ref[0] = ks.astype(BF16)
    kwb_ref[0] = kw.astype(BF16)
    ones = jnp.ones((V_AUG - HEAD_DIM, vs.shape[1]), BF16)
    for g in range(KV_HEADS):
        hs = slice(g * HEAD_DIM, (g + 1) * HEAD_DIM)
        vsb_ref[0, g, :HEAD_DIM] = vs[hs].astype(BF16)
        vsb_ref[0, g, HEAD_DIM:] = ones
        vwb_ref[0, g, :HEAD_DIM] = vw[hs].astype(BF16)
        vwb_ref[0, g, HEAD_DIM:] = ones


def _kv_proj(x, kv_gain, w_kvt, g1, g2, cos_t, sin_t, *, tm):
    bm, sm, d = x.shape
    half = HEAD_DIM // 2
    out_spec = pl.BlockSpec((1, KV_WIDTH, tm), lambda b, j: (b, 0, j))
    aug_spec = pl.BlockSpec((1, KV_HEADS, V_AUG, tm), lambda b, j: (b, 0, 0, j))
    k_bf = jax.ShapeDtypeStruct((bm, KV_WIDTH, sm), BF16)
    v_bf = jax.ShapeDtypeStruct((bm, KV_HEADS, V_AUG, sm), BF16)
    return pl.pallas_call(
        _kv_proj_kernel,
        grid=(bm, sm // tm),
        in_specs=[pl.BlockSpec((1, tm, d), lambda b, j: (b, j, 0)),
                  _const_spec(kv_gain.shape), _const_spec(w_kvt.shape), _const_spec(g1.shape),
                  _const_spec(g2.shape),
                  pl.BlockSpec((half, tm), lambda b, j: (0, j)),
                  pl.BlockSpec((half, tm), lambda b, j: (0, j))],
        out_specs=[out_spec] * 6 + [out_spec, aug_spec, out_spec, aug_spec],
        out_shape=[jax.ShapeDtypeStruct((bm, KV_WIDTH, sm), F32)] * 6 + [k_bf, v_bf, k_bf, v_bf],
        compiler_params=_params(2),
        name="kv_projection",
    )(x, kv_gain, w_kvt, g1, g2, cos_t, sin_t)


def _q_proj_kernel(x_ref, sh_ref, sc_ref, ng_ref, w_ref, qg_ref, cos_ref, sin_ref, q_ref, qr_ref, gates_ref):
    h = (_rms(x_ref[0], ng_ref[...]) * (1.0 + sc_ref[0]) + sh_ref[0]).astype(BF16)
    pt = _dot_nt(w_ref[...], h)
    nq = N_HEADS * HEAD_DIM
    qn, qrot = _head_norm_rope(pt[:nq], qg_ref[...], cos_ref[...], sin_ref[...])
    scale = LOG2E * HEAD_DIM ** -0.5
    q_ref[0] = (qn * scale).T.astype(BF16)
    qr_ref[0] = (qrot * scale).T.astype(BF16)
    gates_ref[0] = jax.nn.sigmoid(pt[nq:]).T


def _q_proj(x, sh, sc, ng, w_int, q_gain, cos_t, sin_t, *, tm):
    bm, sm, d = x.shape
    nq = N_HEADS * HEAD_DIM
    half = HEAD_DIM // 2
    return pl.pallas_call(
        _q_proj_kernel,
        grid=(bm, sm // tm),
        in_specs=[pl.BlockSpec((1, tm, d), lambda b, j: (b, j, 0)),
                  _mod_spec(sh, tm), _mod_spec(sc, tm),
                  _const_spec(ng.shape), _const_spec(w_int.shape), _const_spec(q_gain.shape),
                  pl.BlockSpec((half, tm), lambda b, j: (0, j)),
                  pl.BlockSpec((half, tm), lambda b, j: (0, j))],
        out_specs=[pl.BlockSpec((1, tm, nq), lambda b, j: (b, j, 0)),
                   pl.BlockSpec((1, tm, nq), lambda b, j: (b, j, 0)),
                   pl.BlockSpec((1, tm, GATE_PAD), lambda b, j: (b, j, 0))],
        out_shape=[jax.ShapeDtypeStruct((bm, sm, nq), BF16),
                   jax.ShapeDtypeStruct((bm, sm, nq), BF16),
                   jax.ShapeDtypeStruct((bm, sm, GATE_PAD), F32)],
        compiler_params=_params(2),
        name="q_projection",
    )(x, sh, sc, ng, w_int, q_gain, cos_t, sin_t)


def _store_rows(xs_ref, xt):
    for c in range(xs_ref.shape[0]):
        xs_ref[c] = xt[c * LANES:(c + 1) * LANES, :].T


def _compress_halves(xs_ref, n, pe_a, pe_b, w1a_ref, w1b_ref):
    low = _iota((n, LANES), 1) < HEAD_DIM
    per_head = [[] for _ in range(KV_HEADS)]
    for c in range(KV_WIDTH // LANES):
        for r in range(0, CMP_STRIDE, 2):
            x0 = xs_ref[c, pl.ds(r, n, stride=CMP_STRIDE), :]
            x1 = xs_ref[c, pl.ds(r + 1, n, stride=CMP_STRIDE), :]
            per_head[2 * c].append(jnp.where(low, x0, pltpu.roll(x1, shift=HEAD_DIM, axis=1)))
            per_head[2 * c + 1].append(jnp.where(low, pltpu.roll(x0, shift=HEAD_DIM, axis=1), x1))
    lhs = jnp.concatenate([jnp.concatenate(p, axis=1) for p in per_head], axis=0)
    a = _dot((lhs + pe_a).astype(BF16), w1a_ref[...])
    b = _dot((lhs + pe_b).astype(BF16), w1b_ref[...])
    return a, b


def _compress_finish(a, b_next, b1, w2_ref, b2, gain):
    hid = jax.nn.gelu(a + b_next + b1)
    out = _dot(hid.astype(BF16), w2_ref[...]) + b2
    if gain is not None:
        ms = jnp.sum(out * out, axis=-1, keepdims=True) * (1.0 / HEAD_DIM)
        out = out * lax.rsqrt(ms + EPS) * gain
    return out


def _compress_kernel(x_ref, nxt_ref, pea_ref, peb_ref, w1a_ref, w1b_ref, b1_ref, w2_ref, b2_ref, gain_ref,
                     o_ref, xs_ref, xn_ref, *, norm):
    lc = x_ref.shape[-1]
    n = lc // CMP_STRIDE
    nn = nxt_ref.shape[-1] // CMP_STRIDE
    _store_rows(xs_ref, x_ref[0])
    _store_rows(xn_ref, nxt_ref[0])
    a, b = _compress_halves(xs_ref, n, pea_ref[...], peb_ref[...], w1a_ref, w1b_ref)
    _, bn = _compress_halves(xn_ref, nn, pea_ref[...], peb_ref[...], w1a_ref, w1b_ref)
    last_row = _iota((n, 1), 0) == n - 1
    b_next = jnp.concatenate(
        [jnp.where(last_row, bn[g * nn:g * nn + 1, :], pltpu.roll(b[g * n:(g + 1) * n], shift=n - 1, axis=0))
         for g in range(KV_HEADS)], axis=0)
    out = _compress_finish(a, b_next, b1_ref[...], w2_ref, b2_ref[...], gain_ref[...] if norm else None)
    for g in range(KV_HEADS):
        o_ref[0, g] = out[g * n:(g + 1) * n].T[:HEAD_DIM]


def _compress_prompt(xt, cw, *, norm, lc=2048, ln=256):
    b, _, s = xt.shape
    lc = min(lc, s)
    n = lc // CMP_STRIDE
    n_next = s // ln
    return pl.pallas_call(
        functools.partial(_compress_kernel, norm=norm),
        grid=(b, s // lc),
        in_specs=[pl.BlockSpec((1, KV_WIDTH, lc), lambda i, c: (i, 0, c)),
                  pl.BlockSpec((1, KV_WIDTH, ln),
                               lambda i, c: (i, 0, jnp.minimum((c + 1) * (lc // ln), n_next - 1)))]
                 + [_const_spec(w.shape) for w in cw],
        out_specs=pl.BlockSpec((1, KV_HEADS, HEAD_DIM, n), lambda i, c: (i, 0, 0, c)),
        out_shape=jax.ShapeDtypeStruct((b, KV_HEADS, HEAD_DIM, s // CMP_STRIDE), F32),
        scratch_shapes=[pltpu.VMEM((KV_WIDTH // LANES, lc, LANES), F32),
                        pltpu.VMEM((KV_WIDTH // LANES, ln, LANES), F32)],
        compiler_params=_params(2),
        name="compress_prompt",
    )(xt, xt, *cw)


def _select_blocks(score_t, n_sel):
    srow = _iota(score_t.shape, 1).astype(F32)
    n_blocks = float(score_t.shape[1])

    def pick(_, carry):
        v, sel = carry
        best = jnp.max(v, axis=1, keepdims=True)
        idx = jnp.min(jnp.where(v == best, srow, n_blocks), axis=1, keepdims=True)
        hit = srow == idx
        return jnp.where(hit, KNOCKED_OUT, v), jnp.where(hit, 1.0, sel)

    return lax.fori_loop(0, n_sel, pick, (score_t, jnp.zeros(score_t.shape, F32)))[1]


def _attn_prompt_kernel(q_ref, qr_ref, gates_ref, x_ref, g1_ref, kct_ref, vct_ref, ks_ref, vs_ref, kw_ref,
                        vw_ref, nege_ref, ov_ref, wo_ref, o_ref, *, nc, n_sel, tk):
    n = pl.program_id(1)
    qb = Q_BLOCK
    hq = HEADS_PER_GROUP * qb
    q0 = n * qb
    ncp = kct_ref.shape[-1]
    qpos = q0 + _iota((qb, 1), 0)
    cur = qpos // SLC_BLOCK
    gates = gates_ref[0]
    q_all = q_ref[0]
    qr_all = qr_ref[0]

    ci = _iota((qb, ncp), 1)
    vis = (ci < nc) & (ci * CMP_STRIDE + (CMP_BLOCK - 1) <= qpos)
    vis_bias = jnp.where(vis, 0.0, NEG_INF)
    vis_one = jnp.where(vis, 1.0, 0.0)
    blk = _iota((qb, LANES), 1)
    valid = blk <= cur
    forced = (blk == 0) | (blk == cur) | (blk == cur - 1)
    n_tiles = (q0 + qb + tk - 1) // tk

    def group_rows(x_all, g):
        return jnp.concatenate([x_all[:, h * HEAD_DIM:(h + 1) * HEAD_DIM]
                                for h in range(g * HEADS_PER_GROUP, (g + 1) * HEADS_PER_GROUP)], axis=0)

    ocs, scores = [], []
    for g in range(KV_HEADS):
        s = _dot(group_rows(q_all, g), kct_ref[0, g].astype(BF16)).reshape(HEADS_PER_GROUP, qb, ncp)
        s = s + vis_bias[None]
        p = jnp.exp2(s - jnp.max(s, axis=-1, keepdims=True)) * vis_one[None]
        pc = p * (1.0 / jnp.maximum(jnp.sum(p, axis=-1, keepdims=True), TINY))
        ocs.append(_dot_nt(pc.reshape(hq, ncp).astype(BF16), vct_ref[0, g].astype(BF16)))
        imp = _split_dot(pc[0] + pc[1] + pc[2] + pc[3], ov_ref[...])
        scores.append(jnp.where(valid, jnp.where(forced, FORCE_SCORE, imp), NEG_INF).T)

    sel_t = _select_blocks(jnp.stack(scores), n_sel)

    k_last = (n_tiles - 1) * tk
    causal_bias = jnp.where(k_last + _iota((qb, tk), 1) <= qpos, 0.0, NEG_INF)
    w_tiles = WINDOW // qb + 1
    w_offs, w_biases = [], []
    for t in range(w_tiles):
        off = q0 - WINDOW + qb * t
        w_offs.append(pl.multiple_of(jnp.maximum(off, 0), qb))
        wpos = off + _iota((qb, qb), 1)
        dpos = qpos - wpos
        w_biases.append(jnp.where((dpos >= 0) & (dpos <= WINDOW) & (wpos >= 0), 0.0, NEG_INF))
    w_bias = jnp.concatenate(w_biases, axis=1)
    wk = w_tiles * qb

    qrgs, lhss = [], []
    for g in range(KV_HEADS):
        qrgs.append(group_rows(qr_all, g))
        not_sel = jnp.where(valid, 1.0 - sel_t[g].T, 1.0).astype(BF16)
        lhss.append(jnp.concatenate([jnp.concatenate([not_sel] * HEADS_PER_GROUP, axis=0), qrgs[g]], axis=1))

    def tile(j, carry, causal):
        k0 = pl.multiple_of(j * tk, tk)
        neg = nege_ref[:, pl.ds(k0, tk)]
        out = []
        for g in range(KV_HEADS):
            m_i, acc = carry[g]
            rhs = jnp.concatenate([neg, ks_ref[0, g * HEAD_DIM:(g + 1) * HEAD_DIM, pl.ds(k0, tk)]], axis=0)
            st = _dot(lhss[g], rhs)
            if causal:
                st = (st.reshape(HEADS_PER_GROUP, qb, tk) + causal_bias[None]).reshape(hq, tk)
            m_new = jnp.maximum(m_i, jnp.max(st, axis=-1, keepdims=True))
            pt = jnp.exp2(st - m_new).astype(BF16)
            acc = jnp.exp2(m_i - m_new) * acc + _dot_nt(pt, vs_ref[0, g, :, pl.ds(k0, tk)])
            out.append((m_new, acc))
        return tuple(out)

    init = tuple((jnp.full((hq, 1), NEG_INF, F32), jnp.zeros((hq, V_AUG), F32)) for _ in range(KV_HEADS))
    carry = lax.fori_loop(0, n_tiles - 1, functools.partial(tile, causal=False), init)
    accs = [c[1] for c in tile(n_tiles - 1, carry, True)]

    proj = jnp.zeros((qb, wo_ref.shape[-1]), F32)
    for g in range(KV_HEADS):
        hs = slice(g * HEAD_DIM, (g + 1) * HEAD_DIM)
        qrg = qrgs[g]
        acc_s = accs[g]

        kwt = jnp.concatenate([kw_ref[0, hs, pl.ds(o, qb)] for o in w_offs], axis=1)
        vwt = jnp.concatenate([vw_ref[0, g, :, pl.ds(o, qb)] for o in w_offs], axis=1)
        sw = (_dot(qrg, kwt).reshape(HEADS_PER_GROUP, qb, wk) + w_bias[None]).reshape(hq, wk)
        pw = jnp.exp2(sw - jnp.max(sw, axis=-1, keepdims=True)).astype(BF16)
        acc_w = _dot_nt(pw, vwt)

        for i in range(HEADS_PER_GROUP):
            h = g * HEADS_PER_GROUP + i
            rs = slice(i * qb, (i + 1) * qb)
            gs = gates[:, 3 * h + 1:3 * h + 2] / jnp.maximum(acc_s[rs, HEAD_DIM:HEAD_DIM + 1], TINY)
            gw = gates[:, 3 * h + 2:3 * h + 3] / jnp.maximum(acc_w[rs, HEAD_DIM:HEAD_DIM + 1], TINY)
            o_h = (ocs[g][rs] * gates[:, 3 * h:3 * h + 1] + acc_s[rs, :HEAD_DIM] * gs
                   + acc_w[rs, :HEAD_DIM] * gw)
            proj = proj + _dot(o_h.astype(BF16), wo_ref[h])
    o_ref[0] = x_ref[0] + g1_ref[0] * proj


def _attn_prompt(q, qr, gates, x, gate1, kct, vct, ksb, vsb, kwb, vwb, nege, ov, wo, *, tk=512):
    b, s, d = x.shape
    nq = q.shape[-1]
    ncp = kct.shape[-1]
    nc = s // CMP_STRIDE - 1
    ns = -(-s // SLC_BLOCK)
    tk = min(tk, s)
    kern = functools.partial(_attn_prompt_kernel, nc=nc, n_sel=min(N_SELECT, ns), tk=tk)
    full_k = pl.BlockSpec((1, KV_WIDTH, s), lambda i, j: (i, 0, 0), pipeline_mode=pl.Buffered(1))
    full_v = pl.BlockSpec((1, KV_HEADS, V_AUG, s), lambda i, j: (i, 0, 0, 0), pipeline_mode=pl.Buffered(1))
    cmp_spec = pl.BlockSpec((1, KV_HEADS, HEAD_DIM, ncp), lambda i, j: (i, 0, 0, 0))
    return pl.pallas_call(
        kern,
        grid=(b, s // Q_BLOCK),
        in_specs=[pl.BlockSpec((1, Q_BLOCK, nq), lambda i, j: (i, j, 0)),
                  pl.BlockSpec((1, Q_BLOCK, nq), lambda i, j: (i, j, 0)),
                  pl.BlockSpec((1, Q_BLOCK, GATE_PAD), lambda i, j: (i, j, 0)),
                  pl.BlockSpec((1, Q_BLOCK, d), lambda i, j: (i, j, 0)),
                  pl.BlockSpec((1, 1, d), lambda i, j: (i, 0, 0)),
                  cmp_spec, cmp_spec, full_k, full_v, full_k, full_v,
                  pl.BlockSpec(nege.shape, lambda i, j: (0, 0), pipeline_mode=pl.Buffered(1)),
                  _const_spec(ov.shape), _const_spec(wo.shape)],
        out_specs=pl.BlockSpec((1, Q_BLOCK, d), lambda i, j: (i, j, 0)),
        out_shape=jax.ShapeDtypeStruct((b, s, d), F32),
        compiler_params=_params(2),
        name="nsa_prompt",
    )(q, qr, gates, x, gate1, kct, vct, ksb, vsb, kwb, vwb, nege, ov, wo)


def _attn_sample_kernel(pt_ref, ck_ref, cv_ref, sk_ref, sv_ref, wk_ref, wv_ref, q_ref, qr_ref, gates_ref,
                        ksn_ref, vsn_ref, kwn_ref, vwn_ref, kwc_ref, vwc_ref,
                        kpea_ref, kpeb_ref, kw1a_ref, kw1b_ref, kb1_ref, kw2_ref, kb2_ref, kgain_ref,
                        vpea_ref, vpeb_ref, vw1a_ref, vw1b_ref, vb1_ref, vw2_ref, vb2_ref, vgain_ref,
                        e_ref, ov_ref,
                        o_ref, wko_ref, wvo_ref,
                        slabs, xs_ref, sems, *, past, page, n_sel):
    del vgain_ref
    b = pl.program_id(0)
    slot = b % 2
    caches = (ck_ref, cv_ref, sk_ref, sv_ref)

    def page_copy(seq, slot_, t, p):
        return pltpu.make_async_copy(caches[t].at[pt_ref[seq, p]],
                                     slabs.at[slot_, t, :, pl.ds(p * page, page)], sems.at[slot_, t])

    def gather(seq, slot_, start):
        for t in range(len(caches)):
            for p in range(past // page):
                cp = page_copy(seq, slot_, t, p)
                cp.start() if start else cp.wait()

    @pl.when(b == 0)
    def _():
        gather(0, 0, True)

    @pl.when(b + 1 < pl.num_programs(0))
    def _():
        gather(b + 1, 1 - slot, True)

    gather(b, slot, False)
    sl_ck, sl_cv, sl_sk, sl_sv = (slabs.at[slot, t] for t in range(len(caches)))

    def compute():
        nh = N_HEADS
        qpos = past
        n = past // CMP_STRIDE
        nc = n - 1
        cur = qpos // SLC_BLOCK
        win = wk_ref.shape[-1]
        hrow = _iota((nh, 1), 0) // HEADS_PER_GROUP

        def by_group(fn, width):
            out = jnp.zeros((nh, width), F32)
            for g in range(KV_HEADS):
                out = jnp.where(hrow == g, fn(g), out)
            return out

        def compress(slab, pea, peb, w1a, w1b, b1, w2, b2, gain):
            _store_rows(xs_ref, slab[...])
            a, bh = _compress_halves(xs_ref, n, pea[...], peb[...], w1a, w1b)
            b_next = jnp.concatenate([pltpu.roll(bh[g * n:(g + 1) * n], shift=n - 1, axis=0)
                                      for g in range(KV_HEADS)], axis=0)
            out = _compress_finish(a, b_next, b1[...], w2, b2[...], gain)
            return [out[g * n:(g + 1) * n] for g in range(KV_HEADS)]

        kc = compress(sl_ck, kpea_ref, kpeb_ref, kw1a_ref, kw1b_ref, kb1_ref, kw2_ref, kb2_ref, kgain_ref[...])
        vc = compress(sl_cv, vpea_ref, vpeb_ref, vw1a_ref, vw1b_ref, vb1_ref, vw2_ref, vb2_ref, None)

        q = q_ref[0]
        qr = qr_ref[0]
        qr_lo = qr[:, :HEAD_DIM]
        qr_f = qr_lo.astype(F32)

        sc = by_group(lambda g: _dot_nt(q, kc[g].astype(BF16)), n)
        ci = _iota((nh, n), 1)
        vis = (ci < nc) & (ci * CMP_STRIDE + (CMP_BLOCK - 1) <= qpos)
        sc = jnp.where(vis, sc, NEG_INF)
        pc = jnp.where(vis, jnp.exp2(sc - jnp.max(sc, axis=-1, keepdims=True)), 0.0)
        pc = pc / jnp.maximum(jnp.sum(pc, axis=-1, keepdims=True), TINY)
        o_c = by_group(lambda g: _dot(pc.astype(BF16), vc[g].astype(BF16)), LANES)[:, :HEAD_DIM]

        grow = _iota((nh, 1), 0)
        psum = jnp.zeros((nh, n), F32)
        for g in range(KV_HEADS):
            pg = jnp.sum(pc[g * HEADS_PER_GROUP:(g + 1) * HEADS_PER_GROUP], axis=0, keepdims=True)
            psum = jnp.where(grow == g, pg, psum)
        imp = _split_dot(psum, ov_ref[...])
        blk = _iota((nh, LANES), 1)
        valid = blk <= cur
        forced = (blk == 0) | (blk == cur) | (blk == cur - 1)
        score = jnp.where(valid, jnp.where(forced, FORCE_SCORE, imp), NEG_INF)
        rank = jnp.zeros((nh, LANES), F32)
        for sp in range(cur + 1):
            col = score[:, sp:sp + 1]
            rank = rank + jnp.where(col > score, 1.0, jnp.where(col == score, jnp.where(sp < blk, 1.0, 0.0), 0.0))
        sel_g = jnp.where(valid, jnp.where(rank < n_sel, 1.0, 0.0), 0.0)
        sel = by_group(lambda g: sel_g[g:g + 1, :], LANES)

        chosen = _dot(sel.astype(BF16), e_ref[...])
        kpos = _iota((nh, past), 1)
        bias = jnp.where(kpos <= qpos, jnp.where(chosen > 0.5, 0.0, NEG_INF), NEG_INF)
        ss = by_group(lambda g: _dot(qr_lo, sl_sk[g * HEAD_DIM:(g + 1) * HEAD_DIM, :].astype(BF16)), past) + bias
        s_new = jnp.sum(qr_f * ksn_ref[0], axis=-1, keepdims=True)
        s_new = jnp.where(sel[:, cur:cur + 1] > 0.5, s_new, NEG_INF)
        m = jnp.maximum(jnp.max(ss, axis=-1, keepdims=True), s_new)
        ps = jnp.exp2(ss - m)
        p_new = jnp.exp2(s_new - m)
        den = jnp.maximum(jnp.sum(ps, axis=-1, keepdims=True) + p_new, TINY)
        psb = ps.astype(BF16)
        o_s = by_group(lambda g: _dot_nt(psb, sl_sv[g * HEAD_DIM:(g + 1) * HEAD_DIM, :].astype(BF16)), HEAD_DIM)
        o_s = (o_s + p_new * vsn_ref[0]) / den

        wpos = qpos - win + _iota((nh, win), 1)
        dpos = qpos - wpos
        okw = (dpos >= 0) & (dpos <= WINDOW) & (wpos >= 0)
        sw = by_group(lambda g: _dot(qr_lo, wk_ref[0, g * HEAD_DIM:(g + 1) * HEAD_DIM, :].astype(BF16)), win)
        sw = sw + jnp.where(okw, 0.0, NEG_INF)
        sw_new = jnp.sum(qr_f * kwn_ref[0], axis=-1, keepdims=True)
        mw = jnp.maximum(jnp.max(sw, axis=-1, keepdims=True), sw_new)
        pw = jnp.exp2(sw - mw)
        pw_new = jnp.exp2(sw_new - mw)
        denw = jnp.maximum(jnp.sum(pw, axis=-1, keepdims=True) + pw_new, TINY)
        pwb = pw.astype(BF16)
        o_w = by_group(lambda g: _dot_nt(pwb, wv_ref[0, g * HEAD_DIM:(g + 1) * HEAD_DIM, :].astype(BF16)), HEAD_DIM)
        o_w = (o_w + pw_new * vwn_ref[0]) / denw

        gates = gates_ref[0]
        o_ref[0] = o_c * gates[:, 0:1] + o_s * gates[:, 1:2] + o_w * gates[:, 2:3]

        lane = _iota((KV_WIDTH, win), 1)
        wko_ref[0] = jnp.where(lane == win - 1, kwc_ref[0], pltpu.roll(wk_ref[0], shift=win - 1, axis=1))
        wvo_ref[0] = jnp.where(lane == win - 1, vwc_ref[0], pltpu.roll(wv_ref[0], shift=win - 1, axis=1))

    compute()


def _attn_sample(page_table, caches, win_k, win_v, q, qr, gates, new16, new_cols, cw_k, cw_v, e_mat, ov):
    bd, n_pages = page_table.shape
    page = caches[0].shape[-1]
    past = n_pages * page
    win = win_k.shape[-1]
    cur = past // SLC_BLOCK
    kern = functools.partial(_attn_sample_kernel, past=past, page=page, n_sel=min(N_SELECT, cur + 1))

    def per_seq(a):
        nd = a.ndim
        return pl.BlockSpec((1,) + a.shape[1:], lambda b, pt: (b,) + (0,) * (nd - 1))

    def const(a):
        nd = a.ndim
        return pl.BlockSpec(a.shape, lambda b, pt: (0,) * nd)

    consts = list(cw_k) + list(cw_v) + [e_mat, ov]
    grid_spec = pltpu.PrefetchScalarGridSpec(
        num_scalar_prefetch=1,
        grid=(bd,),
        in_specs=[pl.BlockSpec(memory_space=pl.ANY)] * len(caches)
                 + [per_seq(win_k), per_seq(win_v), per_seq(q), per_seq(qr), per_seq(gates)]
                 + [per_seq(a) for a in new16] + [per_seq(a) for a in new_cols] + [const(a) for a in consts],
        out_specs=[pl.BlockSpec((1, N_HEADS, HEAD_DIM), lambda b, pt: (b, 0, 0)),
                   pl.BlockSpec((1, KV_WIDTH, win), lambda b, pt: (b, 0, 0)),
                   pl.BlockSpec((1, KV_WIDTH, win), lambda b, pt: (b, 0, 0))],
        scratch_shapes=[pltpu.VMEM((2, len(caches), KV_WIDTH, past), F32),
                        pltpu.VMEM((KV_WIDTH // LANES, past, LANES), F32),
                        pltpu.SemaphoreType.DMA((2, len(caches)))],
    )
    return pl.pallas_call(
        kern,
        grid_spec=grid_spec,
        out_shape=[jax.ShapeDtypeStruct((bd, N_HEADS, HEAD_DIM), F32),
                   jax.ShapeDtypeStruct((bd, KV_WIDTH, win), F32),
                   jax.ShapeDtypeStruct((bd, KV_WIDTH, win), F32)],
        compiler_params=_params(1),
        name="nsa_decode",
    )(page_table, *caches, win_k, win_v, q, qr, gates, *new16, *new_cols, *consts)


def _out_proj_kernel(o_ref, x_ref, g_ref, w_ref, y_ref):
    y_ref[0] = x_ref[0] + g_ref[0] * _dot(o_ref[0].astype(BF16), w_ref[...])


def _out_proj(o, x, gate, w):
    bm, sm, d = x.shape
    spec = pl.BlockSpec((1, sm, d), lambda b: (b, 0, 0))
    return pl.pallas_call(
        _out_proj_kernel,
        grid=(bm,),
        in_specs=[pl.BlockSpec((1, sm, o.shape[-1]), lambda b: (b, 0, 0)), spec, spec, _const_spec(w.shape)],
        out_specs=spec,
        out_shape=jax.ShapeDtypeStruct((bm, sm, d), F32),
        compiler_params=_params(1),
        name="attn_out_proj",
    )(o, x, gate, w)


def _rope_tables(pos):
    half = HEAD_DIM // 2
    inv_freq = ROPE_THETA ** (-jnp.arange(half, dtype=F32) / half)
    ang = inv_freq[:, None] * pos.astype(F32)[None, :]
    return jnp.cos(ang), jnp.sin(ang)


def _overlap_matrix(n_rows, nc):
    i = np.arange(n_rows)[:, None] * CMP_STRIDE
    s = np.arange(LANES)[None, :] * SLC_BLOCK
    ov = (i < s + SLC_BLOCK) & (i + CMP_BLOCK > s) & (np.arange(n_rows)[:, None] < nc)
    return jnp.asarray(ov.astype(np.float32), dtype=BF16)


def _expand_matrix(n_keys):
    e = (np.arange(n_keys)[None, :] // SLC_BLOCK) == np.arange(LANES)[:, None]
    return jnp.asarray(e.astype(np.float32), dtype=BF16)


def _compress_weights(t, cmp_pe, cmp_w1, cmp_b1, cmp_w2, cmp_b2, gain):
    half = CMP_STRIDE * HEAD_DIM
    pad = LANES - HEAD_DIM
    return (cmp_pe[t, :CMP_STRIDE].reshape(1, half), cmp_pe[t, CMP_STRIDE:].reshape(1, half),
            cmp_w1[t, :half].astype(BF16), cmp_w1[t, half:].astype(BF16), cmp_b1[t][None, :],
            jnp.pad(cmp_w2[t], ((0, 0), (0, pad))).astype(BF16), jnp.pad(cmp_b2[t], (0, pad))[None, :],
            jnp.pad(gain, (0, pad))[None, :])


def _to_rows(t):
    b, _, n = t.shape
    return t.reshape(b, KV_HEADS, HEAD_DIM, n).transpose(0, 3, 1, 2)


def kernel(x_prompt, x_sample, cache_cmp_k, cache_cmp_v, cache_slc_k, cache_slc_v, state_win_k, state_win_v,
           page_table, c_prompt, c_sample, w_ada, b_ada, norm_g, a_w_in, a_v_gain, a_w_s, a_b_s, a_w_out,
           b_w_in, b_q_gain, b_w_out, kv_gain, w_kv, k_gains, cmp_pe, cmp_w1, cmp_b1, cmp_w2, cmp_b2,
           ffn_w_gu, ffn_w_down):
    b, s, d = x_prompt.shape
    bd = x_sample.shape[0]
    assert x_sample.shape[1] == 1 and w_ada.shape[0] == 2 and a_w_in.shape[0] == 1 and b_w_in.shape[0] == 1
    n_phys, page = cache_cmp_k.shape[:2]
    past = page_table.shape[1] * page
    win = state_win_k.shape[1]
    assert win == WINDOW and s % 512 == 0 and s >= WINDOW and past % SLC_BLOCK == 0
    width = a_w_out.shape[1]
    gd = width // A_GROUPS

    rows = b + bd
    rows_p = -(-rows // 8) * 8
    c_all = jnp.pad(jnp.concatenate([c_prompt, c_sample], axis=0), ((0, rows_p - rows), (0, 0)))
    mod = _ada(c_all, w_ada, b_ada)

    def mods(layer, lo, hi, per_token):
        m = mod[layer, lo:hi].reshape(hi - lo, 6, d)
        return [m[None, :, k, :] if per_token else m[:, None, k, :] for k in range(6)]

    w_in_a = a_w_in[0].astype(BF16)
    w_out_a = a_w_out[0].astype(BF16)
    vg = a_v_gain[0][None, :]
    bias_p = jnp.repeat(a_b_s[0].T, gd, axis=1)
    diag_s = jnp.repeat(a_w_s[0, :, 0, 0], gd)[None, :]
    bias_s = jnp.repeat(a_b_s[0, :, 0], gd)[None, :]
    w_gu = ffn_w_gu.astype(BF16)
    w_down = ffn_w_down.astype(BF16)
    w_kvt = w_kv.T.astype(BF16)
    w_qt = jnp.pad(b_w_in[0].T, ((0, GATE_PAD - 3 * N_HEADS), (0, 0))).astype(BF16)
    w_o = b_w_out[0].astype(BF16)
    kg1 = k_gains[1][:, None]
    kg2 = k_gains[2][:, None]
    qg = b_q_gain[0][:, None]
    cw_k = _compress_weights(0, cmp_pe, cmp_w1, cmp_b1, cmp_w2, cmp_b2, k_gains[0])
    cw_v = _compress_weights(1, cmp_pe, cmp_w1, cmp_b1, cmp_w2, cmp_b2, k_gains[0])
    ng = norm_g[:, :, None, :]

    tm = 512
    sh1, sc1, gt1, sh2, sc2, gt2 = mods(0, 0, b, False)
    x1, v_p = _mixer_a(x_prompt, sh1, sc1, gt1, ng[0, 0], w_in_a, vg, a_w_s[0], bias_p, w_out_a,
                       single_token=False, tm=tm)
    x2 = _ffn(x1, sh2, sc2, gt2, ng[0, 1], w_gu[0], w_down[0], tm=tm)

    cos_p, sin_p = _rope_tables(jnp.arange(s, dtype=jnp.int32))
    kct_p, vct_p, kst_p, vst_p, kwt_p, vwt_p, ksb, vsb, kwb, vwb = _kv_proj(
        x2, kv_gain[None, :], w_kvt, kg1, kg2, cos_p, sin_p, tm=tm)
    kcc = _compress_prompt(kct_p, cw_k, norm=True)
    vcc = _compress_prompt(vct_p, cw_v, norm=False)

    sh1, sc1, gt1, sh2, sc2, gt2 = mods(1, 0, b, False)
    q_p, qr_p, gates_p = _q_proj(x2, sh1, sc1, ng[1, 0], w_qt, qg, cos_p, sin_p, tm=tm)
    ncp = s // CMP_STRIDE
    x3 = _attn_prompt(q_p, qr_p, gates_p, x2, gt1, kcc, vcc, ksb, vsb, kwb, vwb,
                      _expand_matrix(s) * NEG_INF, _overlap_matrix(ncp, ncp - 1),
                      w_o.reshape(N_HEADS, HEAD_DIM, d))
    y_prompt = _ffn(x3, sh2, sc2, gt2, ng[1, 1], w_gu[1], w_down[1], tm=tm)

    xs = x_sample.reshape(1, bd, d)
    sh1, sc1, gt1, sh2, sc2, gt2 = mods(0, b, b + bd, True)
    xs1, v_s = _mixer_a(xs, sh1, sc1, gt1, ng[0, 0], w_in_a, vg, diag_s, bias_s, w_out_a,
                        single_token=True, tm=bd)
    xs2 = _ffn(xs1, sh2, sc2, gt2, ng[0, 1], w_gu[0], w_down[0], tm=bd)

    cos_s, sin_s = _rope_tables(jnp.full((bd,), past, dtype=jnp.int32))
    new_t = _kv_proj(xs2, kv_gain[None, :], w_kvt, kg1, kg2, cos_s, sin_s, tm=bd)[:6]
    sh1, sc1, gt1, sh2, sc2, gt2 = mods(1, b, b + bd, True)
    q_s, qr_s, gates_s = _q_proj(xs2, sh1, sc1, ng[1, 0], w_qt, qg, cos_s, sin_s, tm=bd)

    def per_head(t):
        r = t[0].T.reshape(bd, KV_HEADS, 1, HEAD_DIM)
        return jnp.broadcast_to(r, (bd, KV_HEADS, HEADS_PER_GROUP, HEAD_DIM)).reshape(bd, N_HEADS, HEAD_DIM)

    def lane_pad(t):
        return jnp.pad(t.reshape(bd, N_HEADS, HEAD_DIM), ((0, 0), (0, 0), (0, LANES - HEAD_DIM)))

    caches = [c.transpose(0, 2, 3, 1).reshape(n_phys, KV_WIDTH, page)
              for c in (cache_cmp_k, cache_cmp_v, cache_slc_k, cache_slc_v)]
    win_k = state_win_k.transpose(0, 2, 3, 1).reshape(bd, KV_WIDTH, win)
    win_v = state_win_v.transpose(0, 2, 3, 1).reshape(bd, KV_WIDTH, win)
    new16 = [per_head(new_t[i]) for i in (2, 3, 4, 5)]
    new_cols = [new_t[i][0].T[:, :, None] for i in (4, 5)]
    gates16 = gates_s[0, :, :3 * N_HEADS].reshape(bd, N_HEADS, 3)
    o_s, wk_new, wv_new = _attn_sample(
        page_table, caches, win_k, win_v, lane_pad(q_s), lane_pad(qr_s), gates16, new16, new_cols,
        cw_k, cw_v, _expand_matrix(past), _overlap_matrix(past // CMP_STRIDE, past // CMP_STRIDE - 1))
    xs3 = _out_proj(o_s.reshape(1, bd, N_HEADS * HEAD_DIM), xs2, gt1, w_o)
    y_sample = _ffn(xs3, sh2, sc2, gt2, ng[1, 1], w_gu[1], w_down[1], tm=bd)

    keep = min(WINDOW, s)
    new_rows = [t[0].T.reshape(bd, 1, KV_HEADS, HEAD_DIM) for t in new_t[:4]]
    return (y_prompt, y_sample.reshape(bd, 1, d), v_p[None], v_s.reshape(1, bd, 1, width),
            _to_rows(kct_p), _to_rows(vct_p), _to_rows(kst_p), _to_rows(vst_p),
            _to_rows(kwt_p[:, :, s - keep:]), _to_rows(vwt_p[:, :, s - keep:]),
            new_rows[0], new_rows[1], new_rows[2], new_rows[3],
            _to_rows(wk_new), _to_rows(wv_new))
```

```python
import functools

import numpy as np
import jax
import jax.numpy as jnp
from jax import lax
from jax.experimental import pallas as pl
from jax.experimental.pallas import tpu as pltpu

F32 = jnp.float32
BF16 = jnp.bfloat16

EPS = 1e-6
NEG_INF = -1e30
FORCE_SCORE = 1e9
TINY = 1e-30
ROPE_THETA = 10000.0
LOG2E = 1.4426950408889634
KNOCKED_OUT = -3e38

N_HEADS = 16
HEAD_DIM = 64
KV_HEADS = 4
HEADS_PER_GROUP = N_HEADS // KV_HEADS
KV_WIDTH = KV_HEADS * HEAD_DIM
V_AUG = HEAD_DIM + 16
CHUNK = 128
A_GROUPS = 8
CMP_STRIDE = 16
CMP_BLOCK = 2 * CMP_STRIDE
SLC_BLOCK = 64
N_SELECT = 16
WINDOW = 512
Q_BLOCK = 128
LANES = 128
GATE_PAD = 128
VMEM_LIMIT = 56 * 1024 * 1024

_NT = (((1,), (1,)), ((), ()))


def _params(n_grid):
    return pltpu.CompilerParams(dimension_semantics=("arbitrary",) * n_grid, vmem_limit_bytes=VMEM_LIMIT)


def _dot(a, b):
    return jnp.dot(a, b, preferred_element_type=F32)


def _dot_nt(a, b):
    return lax.dot_general(a, b, _NT, preferred_element_type=F32)


def _rms(x, g):
    return x * lax.rsqrt(jnp.mean(x * x, axis=-1, keepdims=True) + EPS) * g


def _split_dot(x, w):
    hi = x.astype(BF16)
    lo = (x - hi.astype(F32)).astype(BF16)
    return _dot(hi, w) + _dot(lo, w)


def _iota(shape, dim):
    return lax.broadcasted_iota(jnp.int32, shape, dim)


def _const_spec(shape):
    nd = len(shape)
    return pl.BlockSpec(shape, lambda *_: (0,) * nd)


def _mod_spec(m, tm):
    d = m.shape[-1]
    if m.shape[1] == 1:
        return pl.BlockSpec((1, 1, d), lambda b, j: (b, 0, 0))
    return pl.BlockSpec((1, tm, d), lambda b, j: (b, j, 0))


def _ada_kernel(c_ref, w_ref, b_ref, o_ref):
    c = c_ref[...]
    s = (c * jax.nn.sigmoid(c)).astype(BF16)
    o_ref[0] = _dot(s, w_ref[0].astype(BF16)) + b_ref[0]


def _ada(c_all, w_ada, b_ada, tn=1536):
    n_layers, d, d6 = w_ada.shape
    m = c_all.shape[0]
    return pl.pallas_call(
        _ada_kernel,
        grid=(n_layers, d6 // tn),
        in_specs=[pl.BlockSpec((m, d), lambda l, j: (0, 0)),
                  pl.BlockSpec((1, d, tn), lambda l, j: (l, 0, j)),
                  pl.BlockSpec((1, 1, tn), lambda l, j: (l, 0, j))],
        out_specs=pl.BlockSpec((1, m, tn), lambda l, j: (l, 0, j)),
        out_shape=jax.ShapeDtypeStruct((n_layers, m, d6), F32),
        compiler_params=_params(2),
        name="ada_modulation",
    )(c_all, w_ada, b_ada.reshape(n_layers, 1, d6))


def _mixer_a_kernel(x_ref, sh_ref, sc_ref, gt_ref, ng_ref, win_ref, vg_ref, ws_ref, bias_ref, wout_ref,
                    x1_ref, v_ref, um_ref, *, single_token):
    x = x_ref[0]
    h = _rms(x, ng_ref[...]) * (1.0 + sc_ref[0]) + sh_ref[0]
    act = jax.nn.gelu(_dot(h.astype(BF16), win_ref[...]))
    width = act.shape[1] // 2
    u = act[:, :width]
    v = _rms(act[:, width:], vg_ref[...])
    if single_token:
        v_ref[0] = v
        um_ref[...] = (u * (v * ws_ref[...] + bias_ref[...])).astype(BF16)
    else:
        tm = x.shape[0]
        gd = width // A_GROUPS

        @pl.when(pl.program_id(1) == pl.num_programs(1) - 1)
        def _():
            v_ref[0] = v[tm - CHUNK:, :]

        causal = _iota((CHUNK, CHUNK), 0) >= _iota((CHUNK, CHUNK), 1)
        for g in range(A_GROUPS):
            wg = jnp.where(causal, ws_ref[g], 0.0).astype(BF16)
            cs = slice(g * gd, (g + 1) * gd)
            chunks = [slice(c * CHUNK, (c + 1) * CHUNK) for c in range(tm // CHUNK)]
            mixed = _dot(wg, jnp.concatenate([v[rs, cs] for rs in chunks], axis=1).astype(BF16))
            for c, rs in enumerate(chunks):
                um_ref[rs, cs] = (u[rs, cs] * (mixed[:, c * gd:(c + 1) * gd] + bias_ref[:, cs])).astype(BF16)
    x1_ref[0] = x + gt_ref[0] * _dot(um_ref[...], wout_ref[...])


def _mixer_a(x, sh, sc, gt, ng, w_in, v_gain, ws, bias, w_out, *, single_token, tm):
    bm, sm, d = x.shape
    width = w_out.shape[0]
    v_rows = sm if single_token else CHUNK
    kern = functools.partial(_mixer_a_kernel, single_token=single_token)
    return pl.pallas_call(
        kern,
        grid=(bm, sm // tm),
        in_specs=[pl.BlockSpec((1, tm, d), lambda b, j: (b, j, 0)),
                  _mod_spec(sh, tm), _mod_spec(sc, tm), _mod_spec(gt, tm),
                  _const_spec(ng.shape), _const_spec(w_in.shape), _const_spec(v_gain.shape),
                  _const_spec(ws.shape), _const_spec(bias.shape), _const_spec(w_out.shape)],
        out_specs=[pl.BlockSpec((1, tm, d), lambda b, j: (b, j, 0)),
                   pl.BlockSpec((1, v_rows, width), lambda b, j: (b, 0, 0))],
        out_shape=[jax.ShapeDtypeStruct((bm, sm, d), F32),
                   jax.ShapeDtypeStruct((bm, v_rows, width), F32)],
        scratch_shapes=[pltpu.VMEM((tm, width), BF16)],
        compiler_params=_params(2),
        name="mixer_a",
    )(x, sh, sc, gt, ng, w_in, v_gain, ws, bias, w_out)


def _ffn_kernel(x_ref, sh_ref, sc_ref, gt_ref, ng_ref, wgu_ref, wd_ref, o_ref, act_ref, *, fc):
    x = x_ref[0]
    h = (_rms(x, ng_ref[...]) * (1.0 + sc_ref[0]) + sh_ref[0]).astype(BF16)
    dff = wd_ref.shape[0]
    for j in range(dff // fc):
        g = _dot(h, wgu_ref[:, j * fc:(j + 1) * fc])
        u = _dot(h, wgu_ref[:, dff + j * fc:dff + (j + 1) * fc])
        act_ref[:, j * fc:(j + 1) * fc] = (g * jax.nn.sigmoid(g) * u).astype(BF16)
    o_ref[0] = x + gt_ref[0] * _dot(act_ref[...], wd_ref[...])


def _ffn(x, sh, sc, gt, ng, w_gu, w_down, *, tm, fc=256):
    bm, sm, d = x.shape
    dff = w_down.shape[0]
    assert dff % fc == 0
    return pl.pallas_call(
        functools.partial(_ffn_kernel, fc=fc),
        grid=(bm, sm // tm),
        in_specs=[pl.BlockSpec((1, tm, d), lambda b, j: (b, j, 0)),
                  _mod_spec(sh, tm), _mod_spec(sc, tm), _mod_spec(gt, tm),
                  _const_spec(ng.shape), _const_spec(w_gu.shape), _const_spec(w_down.shape)],
        out_specs=pl.BlockSpec((1, tm, d), lambda b, j: (b, j, 0)),
        out_shape=jax.ShapeDtypeStruct((bm, sm, d), F32),
        scratch_shapes=[pltpu.VMEM((tm, dff), BF16)],
        compiler_params=_params(2),
        name="swiglu_ffn",
    )(x, sh, sc, gt, ng, w_gu, w_down)


def _head_norm_rope(t, gain_col, cos, sin):
    rows, n = t.shape
    t3 = t.reshape(rows // HEAD_DIM, HEAD_DIM, n)
    y = t3 * lax.rsqrt(jnp.mean(t3 * t3, axis=1, keepdims=True) + EPS) * gain_col[None]
    half = HEAD_DIM // 2
    y1 = y[:, :half, :]
    y2 = y[:, half:, :]
    c = cos[None]
    s = sin[None]
    rot = jnp.concatenate([y1 * c - y2 * s, y2 * c + y1 * s], axis=1)
    return y.reshape(rows, n), rot.reshape(rows, n)


def _kv_proj_kernel(x_ref, kvg_ref, w_ref, g1_ref, g2_ref, cos_ref, sin_ref,
                    kc_ref, vc_ref, ks_ref, vs_ref, kw_ref, vw_ref, ksb_ref, vsb_ref, kwb_ref, vwb_ref):
    xn = _rms(x_ref[0], kvg_ref[...]).astype(BF16)
    kvt = _dot_nt(w_ref[...], xn)
    cos = cos_ref[...]
    sin = sin_ref[...]

    def part(i):
        return kvt[i * KV_WIDTH:(i + 1) * KV_WIDTH]

    kc_ref[0] = part(0)
    vc_ref[0] = part(1)
    _, ks = _head_norm_rope(part(2), g1_ref[...], cos, sin)
    _, kw = _head_norm_rope(part(4), g2_ref[...], cos, sin)
    vs = part(3)
    vw = part(5)
    ks_ref[0] = ks
    vs_ref[0] = vs
    kw_ref[0] = kw
    vw_ref[0] = vw
    ksb_ref[0] = ks.astype(BF16)
    kwb_ref[0] = kw.astype(BF16)
    ones = jnp.ones((V_AUG - HEAD_DIM, vs.shape[1]), BF16)
    for g in range(KV_HEADS):
        hs = slice(g * HEAD_DIM, (g + 1) * HEAD_DIM)
        vsb_ref[0, g, :HEAD_DIM] = vs[hs].astype(BF16)
        vsb_ref[0, g, HEAD_DIM:] = ones
        vwb_ref[0, g, :HEAD_DIM] = vw[hs].astype(BF16)
        vwb_ref[0, g, HEAD_DIM:] = ones


def _kv_proj(x, kv_gain, w_kvt, g1, g2, cos_t, sin_t, *, tm):
    bm, sm, d = x.shape
    half = HEAD_DIM // 2
    out_spec = pl.BlockSpec((1, KV_WIDTH, tm), lambda b, j: (b, 0, j))
    aug_spec = pl.BlockSpec((1, KV_HEADS, V_AUG, tm), lambda b, j: (b, 0, 0, j))
    k_bf = jax.ShapeDtypeStruct((bm, KV_WIDTH, sm), BF16)
    v_bf = jax.ShapeDtypeStruct((bm, KV_HEADS, V_AUG, sm), BF16)
    return pl.pallas_call(
        _kv_proj_kernel,
        grid=(bm, sm // tm),
        in_specs=[pl.BlockSpec((1, tm, d), lambda b, j: (b, j, 0)),
                  _const_spec(kv_gain.shape), _const_spec(w_kvt.shape), _const_spec(g1.shape),
                  _const_spec(g2.shape),
                  pl.BlockSpec((half, tm), lambda b, j: (0, j)),
                  pl.BlockSpec((half, tm), lambda b, j: (0, j))],
        out_specs=[out_spec] * 6 + [out_spec, aug_spec, out_spec, aug_spec],
        out_shape=[jax.ShapeDtypeStruct((bm, KV_WIDTH, sm), F32)] * 6 + [k_bf, v_bf, k_bf, v_bf],
        compiler_params=_params(2),
        name="kv_projection",
    )(x, kv_gain, w_kvt, g1, g2, cos_t, sin_t)


def _q_proj_kernel(x_ref, sh_ref, sc_ref, ng_ref, w_ref, qg_ref, cos_ref, sin_ref, q_ref, qr_ref, gates_ref):
    h = (_rms(x_ref[0], ng_ref[...]) * (1.0 + sc_ref[0]) + sh_ref[0]).astype(BF16)
    pt = _dot_nt(w_ref[...], h)
    nq = N_HEADS * HEAD_DIM
    qn, qrot = _head_norm_rope(pt[:nq], qg_ref[...], cos_ref[...], sin_ref[...])
    scale = LOG2E * HEAD_DIM ** -0.5
    q_ref[0] = (qn * scale).T.astype(BF16)
    qr_ref[0] = (qrot * scale).T.astype(BF16)
    gates_ref[0] = jax.nn.sigmoid(pt[nq:]).T


def _q_proj(x, sh, sc, ng, w_int, q_gain, cos_t, sin_t, *, tm):
    bm, sm, d = x.shape
    nq = N_HEADS * HEAD_DIM
    half = HEAD_DIM // 2
    return pl.pallas_call(
        _q_proj_kernel,
        grid=(bm, sm // tm),
        in_specs=[pl.BlockSpec((1, tm, d), lambda b, j: (b, j, 0)),
                  _mod_spec(sh, tm), _mod_spec(sc, tm),
                  _const_spec(ng.shape), _const_spec(w_int.shape), _const_spec(q_gain.shape),
                  pl.BlockSpec((half, tm), lambda b, j: (0, j)),
                  pl.BlockSpec((half, tm), lambda b, j: (0, j))],
        out_specs=[pl.BlockSpec((1, tm, nq), lambda b, j: (b, j, 0)),
                   pl.BlockSpec((1, tm, nq), lambda b, j: (b, j, 0)),
                   pl.BlockSpec((1, tm, GATE_PAD), lambda b, j: (b, j, 0))],
        out_shape=[jax.ShapeDtypeStruct((bm, sm, nq), BF16),
                   jax.ShapeDtypeStruct((bm, sm, nq), BF16),
                   jax.ShapeDtypeStruct((bm, sm, GATE_PAD), F32)],
        compiler_params=_params(2),
        name="q_projection",
    )(x, sh, sc, ng, w_int, q_gain, cos_t, sin_t)


def _store_rows(xs_ref, xt):
    for c in range(xs_ref.shape[0]):
        xs_ref[c] = xt[c * LANES:(c + 1) * LANES, :].T


def _compress_halves(xs_ref, n, pe_a, pe_b, w1a_ref, w1b_ref):
    low = _iota((n, LANES), 1) < HEAD_DIM
    per_head = [[] for _ in range(KV_HEADS)]
    for c in range(KV_WIDTH // LANES):
        for r in range(0, CMP_STRIDE, 2):
            x0 = xs_ref[c, pl.ds(r, n, stride=CMP_STRIDE), :]
            x1 = xs_ref[c, pl.ds(r + 1, n, stride=CMP_STRIDE), :]
            per_head[2 * c].append(jnp.where(low, x0, pltpu.roll(x1, shift=HEAD_DIM, axis=1)))
            per_head[2 * c + 1].append(jnp.where(low, pltpu.roll(x0, shift=HEAD_DIM, axis=1), x1))
    lhs = jnp.concatenate([jnp.concatenate(p, axis=1) for p in per_head], axis=0)
    a = _dot((lhs + pe_a).astype(BF16), w1a_ref[...])
    b = _dot((lhs + pe_b).astype(BF16), w1b_ref[...])
    return a, b


def _compress_finish(a, b_next, b1, w2_ref, b2, gain):
    hid = jax.nn.gelu(a + b_next + b1)
    out = _dot(hid.astype(BF16), w2_ref[...]) + b2
    if gain is not None:
        ms = jnp.sum(out * out, axis=-1, keepdims=True) * (1.0 / HEAD_DIM)
        out = out * lax.rsqrt(ms + EPS) * gain
    return out


def _compress_kernel(x_ref, nxt_ref, pea_ref, peb_ref, w1a_ref, w1b_ref, b1_ref, w2_ref, b2_ref, gain_ref,
                     o_ref, xs_ref, xn_ref, *, norm):
    lc = x_ref.shape[-1]
    n = lc // CMP_STRIDE
    nn = nxt_ref.shape[-1] // CMP_STRIDE
    _store_rows(xs_ref, x_ref[0])
    _store_rows(xn_ref, nxt_ref[0])
    a, b = _compress_halves(xs_ref, n, pea_ref[...], peb_ref[...], w1a_ref, w1b_ref)
    _, bn = _compress_halves(xn_ref, nn, pea_ref[...], peb_ref[...], w1a_ref, w1b_ref)
    last_row = _iota((n, 1), 0) == n - 1
    b_next = jnp.concatenate(
        [jnp.where(last_row, bn[g * nn:g * nn + 1, :], pltpu.roll(b[g * n:(g + 1) * n], shift=n - 1, axis=0))
         for g in range(KV_HEADS)], axis=0)
    out = _compress_finish(a, b_next, b1_ref[...], w2_ref, b2_ref[...], gain_ref[...] if norm else None)
    for g in range(KV_HEADS):
        o_ref[0, g] = out[g * n:(g + 1) * n].T[:HEAD_DIM]


def _compress_prompt(xt, cw, *, norm, lc=2048, ln=256):
    b, _, s = xt.shape
    lc = min(lc, s)
    n = lc // CMP_STRIDE
    n_next = s // ln
    return pl.pallas_call(
        functools.partial(_compress_kernel, norm=norm),
        grid=(b, s // lc),
        in_specs=[pl.BlockSpec((1, KV_WIDTH, lc), lambda i, c: (i, 0, c)),
                  pl.BlockSpec((1, KV_WIDTH, ln),
                               lambda i, c: (i, 0, jnp.minimum((c + 1) * (lc // ln), n_next - 1)))]
                 + [_const_spec(w.shape) for w in cw],
        out_specs=pl.BlockSpec((1, KV_HEADS, HEAD_DIM, n), lambda i, c: (i, 0, 0, c)),
        out_shape=jax.ShapeDtypeStruct((b, KV_HEADS, HEAD_DIM, s // CMP_STRIDE), F32),
        scratch_shapes=[pltpu.VMEM((KV_WIDTH // LANES, lc, LANES), F32),
                        pltpu.VMEM((KV_WIDTH // LANES, ln, LANES), F32)],
        compiler_params=_params(2),
        name="compress_prompt",
    )(xt, xt, *cw)


def _select_blocks(scores_t, n_sel):
    srow = _iota(scores_t[0].shape, 0).astype(F32)
    n_blocks = float(scores_t[0].shape[0])
    vals = list(scores_t)
    sels = [jnp.zeros(v.shape, F32) for v in vals]
    for _ in range(n_sel):
        for g, v in enumerate(vals):
            best = jnp.max(v, axis=0, keepdims=True)
            idx = jnp.min(jnp.where(v == best, srow, n_blocks), axis=0, keepdims=True)
            hit = srow == idx
            vals[g] = jnp.where(hit, KNOCKED_OUT, v)
            sels[g] = jnp.where(hit, 1.0, sels[g])
    return sels


def _attn_prompt_kernel(q_ref, qr_ref, gates_ref, x_ref, g1_ref, kct_ref, vct_ref, ks_ref, vs_ref, kw_ref,
                        vw_ref, nege_ref, ov_ref, wo_ref, o_ref, *, nc, n_sel, tk):
    n = pl.program_id(1)
    qb = Q_BLOCK
    hq = HEADS_PER_GROUP * qb
    q0 = n * qb
    ncp = kct_ref.shape[-1]
    qpos = q0 + _iota((qb, 1), 0)
    cur = qpos // SLC_BLOCK
    gates = gates_ref[0]
    q_all = q_ref[0]
    qr_all = qr_ref[0]

    ci = _iota((qb, ncp), 1)
    vis = (ci < nc) & (ci * CMP_STRIDE + (CMP_BLOCK - 1) <= qpos)
    vis_bias = jnp.where(vis, 0.0, NEG_INF)
    vis_one = jnp.where(vis, 1.0, 0.0)
    blk = _iota((qb, LANES), 1)
    valid = blk <= cur
    forced = (blk == 0) | (blk == cur) | (blk == cur - 1)
    n_tiles = (q0 + qb + tk - 1) // tk

    def group_rows(x_all, g):
        return jnp.concatenate([x_all[:, h * HEAD_DIM:(h + 1) * HEAD_DIM]
                                for h in range(g * HEADS_PER_GROUP, (g + 1) * HEADS_PER_GROUP)], axis=0)

    def cmp_branch(w):
        def run():
            ocs_, imps_ = [], []
            for g in range(KV_HEADS):
                s = _dot(group_rows(q_all, g), kct_ref[0, g, :, :w].astype(BF16)).reshape(HEADS_PER_GROUP, qb, w)
                s = s + vis_bias[None, :, :w]
                p = jnp.exp2(s - jnp.max(s, axis=-1, keepdims=True)) * vis_one[None, :, :w]
                pc = p * (1.0 / jnp.maximum(jnp.sum(p, axis=-1, keepdims=True), TINY))
                ocs_.append(_dot_nt(pc.reshape(hq, w).astype(BF16), vct_ref[0, g, :, :w].astype(BF16)))
                imps_.append(_split_dot(pc[0] + pc[1] + pc[2] + pc[3], ov_ref[:w, :]))
            return tuple(ocs_) + tuple(imps_)
        return run

    widths = list(range(LANES, ncp, LANES)) + [ncp]
    n_vis = (q0 + qb - CMP_BLOCK) // CMP_STRIDE + 1
    cmp_out = lax.switch(jnp.clip((n_vis + LANES - 1) // LANES - 1, 0, len(widths) - 1),
                         [cmp_branch(w) for w in widths])
    ocs, imps = cmp_out[:KV_HEADS], cmp_out[KV_HEADS:]

    k_last = (n_tiles - 1) * tk
    causal_bias = jnp.where(k_last + _iota((qb, tk), 1) <= qpos, 0.0, NEG_INF)
    w_tiles = WINDOW // qb + 1
    w_offs, w_biases = [], []
    for t in range(w_tiles):
        off = q0 - WINDOW + qb * t
        w_offs.append(pl.multiple_of(jnp.maximum(off, 0), qb))
        wpos = off + _iota((qb, qb), 1)
        dpos = qpos - wpos
        w_biases.append(jnp.where((dpos >= 0) & (dpos <= WINDOW) & (wpos >= 0), 0.0, NEG_INF))
    w_bias = jnp.concatenate(w_biases, axis=1)
    wk = w_tiles * qb

    qrgs = [group_rows(qr_all, g) for g in range(KV_HEADS)]
    acc_ws = []
    for g in range(KV_HEADS):
        hs = slice(g * HEAD_DIM, (g + 1) * HEAD_DIM)
        kwt = jnp.concatenate([kw_ref[0, hs, pl.ds(o, qb)] for o in w_offs], axis=1)
        vwt = jnp.concatenate([vw_ref[0, g, :, pl.ds(o, qb)] for o in w_offs], axis=1)
        sw = (_dot(qrgs[g], kwt).reshape(HEADS_PER_GROUP, qb, wk) + w_bias[None]).reshape(hq, wk)
        pw = jnp.exp2(sw - jnp.max(sw, axis=-1, keepdims=True)).astype(BF16)
        acc_ws.append(_dot_nt(pw, vwt))

    scores = [jnp.where(valid, jnp.where(forced, KNOCKED_OUT, imp), NEG_INF).T for imp in imps]
    sel_t = _select_blocks(scores, n_sel - 3)
    forced_one = jnp.where(forced, 1.0, 0.0)

    lhss = []
    for g in range(KV_HEADS):
        not_sel = jnp.where(valid, 1.0 - jnp.maximum(sel_t[g].T, forced_one), 1.0).astype(BF16)
        lhss.append(jnp.concatenate([jnp.concatenate([not_sel] * HEADS_PER_GROUP, axis=0), qrgs[g]], axis=1))

    def tile(k0, width, carry, causal):
        neg = nege_ref[:, pl.ds(k0, width)]
        out = []
        for g in range(KV_HEADS):
            m_i, acc = carry[g]
            rhs = jnp.concatenate([neg, ks_ref[0, g * HEAD_DIM:(g + 1) * HEAD_DIM, pl.ds(k0, width)]], axis=0)
            st = _dot(lhss[g], rhs)
            if causal:
                st = (st.reshape(HEADS_PER_GROUP, qb, width) + causal_bias[None]).reshape(hq, width)
            m_new = jnp.maximum(m_i, jnp.max(st, axis=-1, keepdims=True))
            pt = jnp.exp2(st - m_new).astype(BF16)
            acc = jnp.exp2(m_i - m_new) * acc + _dot_nt(pt, vs_ref[0, g, :, pl.ds(k0, width)])
            out.append((m_new, acc))
        return tuple(out)

    n_full = n_tiles - 1
    odd = n_full % 2
    init = tuple((jnp.full((hq, 1), NEG_INF, F32), jnp.zeros((hq, V_AUG), F32)) for _ in range(KV_HEADS))
    carry = lax.cond(odd == 1, lambda c: tile(0, tk, c, False), lambda c: c, init)
    carry = lax.fori_loop(
        0, n_full // 2,
        lambda j, c: tile(pl.multiple_of((odd + 2 * j) * tk, tk), 2 * tk, c, False), carry)
    accs = [c[1] for c in tile(pl.multiple_of(n_full * tk, tk), tk, carry, True)]

    proj = jnp.zeros((qb, wo_ref.shape[-1]), F32)
    for g in range(KV_HEADS):
        acc_s = accs[g]
        acc_w = acc_ws[g]
        for i in range(HEADS_PER_GROUP):
            h = g * HEADS_PER_GROUP + i
            rs = slice(i * qb, (i + 1) * qb)
            gs = gates[:, 3 * h + 1:3 * h + 2] / jnp.maximum(acc_s[rs, HEAD_DIM:HEAD_DIM + 1], TINY)
            gw = gates[:, 3 * h + 2:3 * h + 3] / jnp.maximum(acc_w[rs, HEAD_DIM:HEAD_DIM + 1], TINY)
            o_h = (ocs[g][rs] * gates[:, 3 * h:3 * h + 1] + acc_s[rs, :HEAD_DIM] * gs
                   + acc_w[rs, :HEAD_DIM] * gw)
            proj = proj + _dot(o_h.astype(BF16), wo_ref[h])
    o_ref[0] = x_ref[0] + g1_ref[0] * proj


def _attn_prompt(q, qr, gates, x, gate1, kct, vct, ksb, vsb, kwb, vwb, nege, ov, wo, *, tk=512):
    b, s, d = x.shape
    nq = q.shape[-1]
    ncp = kct.shape[-1]
    nc = s // CMP_STRIDE - 1
    ns = -(-s // SLC_BLOCK)
    tk = min(tk, s)
    kern = functools.partial(_attn_prompt_kernel, nc=nc, n_sel=min(N_SELECT, ns), tk=tk)
    full_k = pl.BlockSpec((1, KV_WIDTH, s), lambda i, j: (i, 0, 0), pipeline_mode=pl.Buffered(1))
    full_v = pl.BlockSpec((1, KV_HEADS, V_AUG, s), lambda i, j: (i, 0, 0, 0), pipeline_mode=pl.Buffered(1))
    cmp_spec = pl.BlockSpec((1, KV_HEADS, HEAD_DIM, ncp), lambda i, j: (i, 0, 0, 0))
    return pl.pallas_call(
        kern,
        grid=(b, s // Q_BLOCK),
        in_specs=[pl.BlockSpec((1, Q_BLOCK, nq), lambda i, j: (i, j, 0)),
                  pl.BlockSpec((1, Q_BLOCK, nq), lambda i, j: (i, j, 0)),
                  pl.BlockSpec((1, Q_BLOCK, GATE_PAD), lambda i, j: (i, j, 0)),
                  pl.BlockSpec((1, Q_BLOCK, d), lambda i, j: (i, j, 0)),
                  pl.BlockSpec((1, 1, d), lambda i, j: (i, 0, 0)),
                  cmp_spec, cmp_spec, full_k, full_v, full_k, full_v,
                  pl.BlockSpec(nege.shape, lambda i, j: (0, 0), pipeline_mode=pl.Buffered(1)),
                  _const_spec(ov.shape), _const_spec(wo.shape)],
        out_specs=pl.BlockSpec((1, Q_BLOCK, d), lambda i, j: (i, j, 0)),
        out_shape=jax.ShapeDtypeStruct((b, s, d), F32),
        compiler_params=_params(2),
        name="nsa_prompt",
    )(q, qr, gates, x, gate1, kct, vct, ksb, vsb, kwb, vwb, nege, ov, wo)


def _attn_sample_kernel(pt_ref, ck_ref, cv_ref, sk_ref, sv_ref, wk_ref, wv_ref, q_ref, qr_ref, gates_ref,
                        ksn_ref, vsn_ref, kwn_ref, vwn_ref, kwc_ref, vwc_ref,
                        kpea_ref, kpeb_ref, kw1a_ref, kw1b_ref, kb1_ref, kw2_ref, kb2_ref, kgain_ref,
                        vpea_ref, vpeb_ref, vw1a_ref, vw1b_ref, vb1_ref, vw2_ref, vb2_ref, vgain_ref,
                        e_ref, ov_ref,
                        o_ref, wko_ref, wvo_ref,
                        slabs, xs_ref, sems, *, past, page, n_sel):
    del vgain_ref
    b = pl.program_id(0)
    slot = b % 2
    caches = (ck_ref, cv_ref, sk_ref, sv_ref)

    def page_copy(seq, slot_, t, p):
        return pltpu.make_async_copy(caches[t].at[pt_ref[seq, p]],
                                     slabs.at[slot_, t, :, pl.ds(p * page, page)], sems.at[slot_, t])

    def gather(seq, slot_, start):
        for t in range(len(caches)):
            for p in range(past // page):
                cp = page_copy(seq, slot_, t, p)
                cp.start() if start else cp.wait()

    @pl.when(b == 0)
    def _():
        gather(0, 0, True)

    @pl.when(b + 1 < pl.num_programs(0))
    def _():
        gather(b + 1, 1 - slot, True)

    gather(b, slot, False)
    sl_ck, sl_cv, sl_sk, sl_sv = (slabs.at[slot, t] for t in range(len(caches)))

    def compute():
        nh = N_HEADS
        qpos = past
        n = past // CMP_STRIDE
        nc = n - 1
        cur = qpos // SLC_BLOCK
        win = wk_ref.shape[-1]
        hrow = _iota((nh, 1), 0) // HEADS_PER_GROUP

        def by_group(fn, width):
            out = jnp.zeros((nh, width), F32)
            for g in range(KV_HEADS):
                out = jnp.where(hrow == g, fn(g), out)
            return out

        def compress(slab, pea, peb, w1a, w1b, b1, w2, b2, gain):
            _store_rows(xs_ref, slab[...])
            a, bh = _compress_halves(xs_ref, n, pea[...], peb[...], w1a, w1b)
            b_next = jnp.concatenate([pltpu.roll(bh[g * n:(g + 1) * n], shift=n - 1, axis=0)
                                      for g in range(KV_HEADS)], axis=0)
            out = _compress_finish(a, b_next, b1[...], w2, b2[...], gain)
            return [out[g * n:(g + 1) * n] for g in range(KV_HEADS)]

        kc = compress(sl_ck, kpea_ref, kpeb_ref, kw1a_ref, kw1b_ref, kb1_ref, kw2_ref, kb2_ref, kgain_ref[...])
        vc = compress(sl_cv, vpea_ref, vpeb_ref, vw1a_ref, vw1b_ref, vb1_ref, vw2_ref, vb2_ref, None)

        q = q_ref[0]
        qr = qr_ref[0]
        qr_lo = qr[:, :HEAD_DIM]
        qr_f = qr_lo.astype(F32)

        sc = by_group(lambda g: _dot_nt(q, kc[g].astype(BF16)), n)
        ci = _iota((nh, n), 1)
        vis = (ci < nc) & (ci * CMP_STRIDE + (CMP_BLOCK - 1) <= qpos)
        sc = jnp.where(vis, sc, NEG_INF)
        pc = jnp.where(vis, jnp.exp2(sc - jnp.max(sc, axis=-1, keepdims=True)), 0.0)
        pc = pc / jnp.maximum(jnp.sum(pc, axis=-1, keepdims=True), TINY)
        o_c = by_group(lambda g: _dot(pc.astype(BF16), vc[g].astype(BF16)), LANES)[:, :HEAD_DIM]

        grow = _iota((nh, 1), 0)
        psum = jnp.zeros((nh, n), F32)
        for g in range(KV_HEADS):
            pg = jnp.sum(pc[g * HEADS_PER_GROUP:(g + 1) * HEADS_PER_GROUP], axis=0, keepdims=True)
            psum = jnp.where(grow == g, pg, psum)
        imp = _split_dot(psum, ov_ref[...])
        blk = _iota((nh, LANES), 1)
        valid = blk <= cur
        forced = (blk == 0) | (blk == cur) | (blk == cur - 1)
        score = jnp.where(valid, jnp.where(forced, FORCE_SCORE, imp), NEG_INF)
        rank = jnp.zeros((nh, LANES), F32)
        for sp in range(cur + 1):
            col = score[:, sp:sp + 1]
            rank = rank + jnp.where(col > score, 1.0, jnp.where(col == score, jnp.where(sp < blk, 1.0, 0.0), 0.0))
        sel_g = jnp.where(valid, jnp.where(rank < n_sel, 1.0, 0.0), 0.0)
        sel = by_group(lambda g: sel_g[g:g + 1, :], LANES)

        chosen = _dot(sel.astype(BF16), e_ref[...])
        kpos = _iota((nh, past), 1)
        bias = jnp.where(kpos <= qpos, jnp.where(chosen > 0.5, 0.0, NEG_INF), NEG_INF)
        ss = by_group(lambda g: _dot(qr_lo, sl_sk[g * HEAD_DIM:(g + 1) * HEAD_DIM, :].astype(BF16)), past) + bias
        s_new = jnp.sum(qr_f * ksn_ref[0], axis=-1, keepdims=True)
        s_new = jnp.where(sel[:, cur:cur + 1] > 0.5, s_new, NEG_INF)
        m = jnp.maximum(jnp.max(ss, axis=-1, keepdims=True), s_new)
        ps = jnp.exp2(ss - m)
        p_new = jnp.exp2(s_new - m)
        den = jnp.maximum(jnp.sum(ps, axis=-1, keepdims=True) + p_new, TINY)
        psb = ps.astype(BF16)
        o_s = by_group(lambda g: _dot_nt(psb, sl_sv[g * HEAD_DIM:(g + 1) * HEAD_DIM, :].astype(BF16)), HEAD_DIM)
        o_s = (o_s + p_new * vsn_ref[0]) / den

        wpos = qpos - win + _iota((nh, win), 1)
        dpos = qpos - wpos
        okw = (dpos >= 0) & (dpos <= WINDOW) & (wpos >= 0)
        sw = by_group(lambda g: _dot(qr_lo, wk_ref[0, g * HEAD_DIM:(g + 1) * HEAD_DIM, :].astype(BF16)), win)
        sw = sw + jnp.where(okw, 0.0, NEG_INF)
        sw_new = jnp.sum(qr_f * kwn_ref[0], axis=-1, keepdims=True)
        mw = jnp.maximum(jnp.max(sw, axis=-1, keepdims=True), sw_new)
        pw = jnp.exp2(sw - mw)
        pw_new = jnp.exp2(sw_new - mw)
        denw = jnp.maximum(jnp.sum(pw, axis=-1, keepdims=True) + pw_new, TINY)
        pwb = pw.astype(BF16)
        o_w = by_group(lambda g: _dot_nt(pwb, wv_ref[0, g * HEAD_DIM:(g + 1) * HEAD_DIM, :].astype(BF16)), HEAD_DIM)
        o_w = (o_w + pw_new * vwn_ref[0]) / denw

        gates = gates_ref[0]
        o_ref[0] = o_c * gates[:, 0:1] + o_s * gates[:, 1:2] + o_w * gates[:, 2:3]

        lane = _iota((KV_WIDTH, win), 1)
        wko_ref[0] = jnp.where(lane == win - 1, kwc_ref[0], pltpu.roll(wk_ref[0], shift=win - 1, axis=1))
        wvo_ref[0] = jnp.where(lane == win - 1, vwc_ref[0], pltpu.roll(wv_ref[0], shift=win - 1, axis=1))

    compute()


def _attn_sample(page_table, caches, win_k, win_v, q, qr, gates, new16, new_cols, cw_k, cw_v, e_mat, ov):
    bd, n_pages = page_table.shape
    page = caches[0].shape[-1]
    past = n_pages * page
    win = win_k.shape[-1]
    cur = past // SLC_BLOCK
    kern = functools.partial(_attn_sample_kernel, past=past, page=page, n_sel=min(N_SELECT, cur + 1))

    def per_seq(a):
        nd = a.ndim
        return pl.BlockSpec((1,) + a.shape[1:], lambda b, pt: (b,) + (0,) * (nd - 1))

    def const(a):
        nd = a.ndim
        return pl.BlockSpec(a.shape, lambda b, pt: (0,) * nd)

    consts = list(cw_k) + list(cw_v) + [e_mat, ov]
    grid_spec = pltpu.PrefetchScalarGridSpec(
        num_scalar_prefetch=1,
        grid=(bd,),
        in_specs=[pl.BlockSpec(memory_space=pl.ANY)] * len(caches)
                 + [per_seq(win_k), per_seq(win_v), per_seq(q), per_seq(qr), per_seq(gates)]
                 + [per_seq(a) for a in new16] + [per_seq(a) for a in new_cols] + [const(a) for a in consts],
        out_specs=[pl.BlockSpec((1, N_HEADS, HEAD_DIM), lambda b, pt: (b, 0, 0)),
                   pl.BlockSpec((1, KV_WIDTH, win), lambda b, pt: (b, 0, 0)),
                   pl.BlockSpec((1, KV_WIDTH, win), lambda b, pt: (b, 0, 0))],
        scratch_shapes=[pltpu.VMEM((2, len(caches), KV_WIDTH, past), F32),
                        pltpu.VMEM((KV_WIDTH // LANES, past, LANES), F32),
                        pltpu.SemaphoreType.DMA((2, len(caches)))],
    )
    return pl.pallas_call(
        kern,
        grid_spec=grid_spec,
        out_shape=[jax.ShapeDtypeStruct((bd, N_HEADS, HEAD_DIM), F32),
                   jax.ShapeDtypeStruct((bd, KV_WIDTH, win), F32),
                   jax.ShapeDtypeStruct((bd, KV_WIDTH, win), F32)],
        compiler_params=_params(1),
        name="nsa_decode",
    )(page_table, *caches, win_k, win_v, q, qr, gates, *new16, *new_cols, *consts)


def _out_proj_kernel(o_ref, x_ref, g_ref, w_ref, y_ref):
    y_ref[0] = x_ref[0] + g_ref[0] * _dot(o_ref[0].astype(BF16), w_ref[...])


def _out_proj(o, x, gate, w):
    bm, sm, d = x.shape
    spec = pl.BlockSpec((1, sm, d), lambda b: (b, 0, 0))
    return pl.pallas_call(
        _out_proj_kernel,
        grid=(bm,),
        in_specs=[pl.BlockSpec((1, sm, o.shape[-1]), lambda b: (b, 0, 0)), spec, spec, _const_spec(w.shape)],
        out_specs=spec,
        out_shape=jax.ShapeDtypeStruct((bm, sm, d), F32),
        compiler_params=_params(1),
        name="attn_out_proj",
    )(o, x, gate, w)


def _rope_tables(pos):
    half = HEAD_DIM // 2
    inv_freq = ROPE_THETA ** (-jnp.arange(half, dtype=F32) / half)
    ang = inv_freq[:, None] * pos.astype(F32)[None, :]
    return jnp.cos(ang), jnp.sin(ang)


def _overlap_matrix(n_rows, nc):
    i = np.arange(n_rows)[:, None] * CMP_STRIDE
    s = np.arange(LANES)[None, :] * SLC_BLOCK
    ov = (i < s + SLC_BLOCK) & (i + CMP_BLOCK > s) & (np.arange(n_rows)[:, None] < nc)
    return jnp.asarray(ov.astype(np.float32), dtype=BF16)


def _expand_matrix(n_keys):
    e = (np.arange(n_keys)[None, :] // SLC_BLOCK) == np.arange(LANES)[:, None]
    return jnp.asarray(e.astype(np.float32), dtype=BF16)


def _compress_weights(t, cmp_pe, cmp_w1, cmp_b1, cmp_w2, cmp_b2, gain):
    half = CMP_STRIDE * HEAD_DIM
    pad = LANES - HEAD_DIM
    return (cmp_pe[t, :CMP_STRIDE].reshape(1, half), cmp_pe[t, CMP_STRIDE:].reshape(1, half),
            cmp_w1[t, :half].astype(BF16), cmp_w1[t, half:].astype(BF16), cmp_b1[t][None, :],
            jnp.pad(cmp_w2[t], ((0, 0), (0, pad))).astype(BF16), jnp.pad(cmp_b2[t], (0, pad))[None, :],
            jnp.pad(gain, (0, pad))[None, :])


def _to_rows(t):
    b, _, n = t.shape
    return t.reshape(b, KV_HEADS, HEAD_DIM, n).transpose(0, 3, 1, 2)


def kernel(x_prompt, x_sample, cache_cmp_k, cache_cmp_v, cache_slc_k, cache_slc_v, state_win_k, state_win_v,
           page_table, c_prompt, c_sample, w_ada, b_ada, norm_g, a_w_in, a_v_gain, a_w_s, a_b_s, a_w_out,
           b_w_in, b_q_gain, b_w_out, kv_gain, w_kv, k_gains, cmp_pe, cmp_w1, cmp_b1, cmp_w2, cmp_b2,
           ffn_w_gu, ffn_w_down):
    b, s, d = x_prompt.shape
    bd = x_sample.shape[0]
    assert x_sample.shape[1] == 1 and w_ada.shape[0] == 2 and a_w_in.shape[0] == 1 and b_w_in.shape[0] == 1
    n_phys, page = cache_cmp_k.shape[:2]
    past = page_table.shape[1] * page
    win = state_win_k.shape[1]
    assert win == WINDOW and s % 512 == 0 and s >= WINDOW and past % SLC_BLOCK == 0
    width = a_w_out.shape[1]
    gd = width // A_GROUPS

    rows = b + bd
    rows_p = -(-rows // 8) * 8
    c_all = jnp.pad(jnp.concatenate([c_prompt, c_sample], axis=0), ((0, rows_p - rows), (0, 0)))
    mod = _ada(c_all, w_ada, b_ada)

    def mods(layer, lo, hi, per_token):
        m = mod[layer, lo:hi].reshape(hi - lo, 6, d)
        return [m[None, :, k, :] if per_token else m[:, None, k, :] for k in range(6)]

    w_in_a = a_w_in[0].astype(BF16)
    w_out_a = a_w_out[0].astype(BF16)
    vg = a_v_gain[0][None, :]
    bias_p = jnp.repeat(a_b_s[0].T, gd, axis=1)
    diag_s = jnp.repeat(a_w_s[0, :, 0, 0], gd)[None, :]
    bias_s = jnp.repeat(a_b_s[0, :, 0], gd)[None, :]
    w_gu = ffn_w_gu.astype(BF16)
    w_down = ffn_w_down.astype(BF16)
    w_kvt = w_kv.T.astype(BF16)
    w_qt = jnp.pad(b_w_in[0].T, ((0, GATE_PAD - 3 * N_HEADS), (0, 0))).astype(BF16)
    w_o = b_w_out[0].astype(BF16)
    kg1 = k_gains[1][:, None]
    kg2 = k_gains[2][:, None]
    qg = b_q_gain[0][:, None]
    cw_k = _compress_weights(0, cmp_pe, cmp_w1, cmp_b1, cmp_w2, cmp_b2, k_gains[0])
    cw_v = _compress_weights(1, cmp_pe, cmp_w1, cmp_b1, cmp_w2, cmp_b2, k_gains[0])
    ng = norm_g[:, :, None, :]

    tm = 512
    sh1, sc1, gt1, sh2, sc2, gt2 = mods(0, 0, b, False)
    x1, v_p = _mixer_a(x_prompt, sh1, sc1, gt1, ng[0, 0], w_in_a, vg, a_w_s[0], bias_p, w_out_a,
                       single_token=False, tm=tm)
    x2 = _ffn(x1, sh2, sc2, gt2, ng[0, 1], w_gu[0], w_down[0], tm=tm)

    cos_p, sin_p = _rope_tables(jnp.arange(s, dtype=jnp.int32))
    kct_p, vct_p, kst_p, vst_p, kwt_p, vwt_p, ksb, vsb, kwb, vwb = _kv_proj(
        x2, kv_gain[None, :], w_kvt, kg1, kg2, cos_p, sin_p, tm=tm)
    kcc = _compress_prompt(kct_p, cw_k, norm=True)
    vcc = _compress_prompt(vct_p, cw_v, norm=False)

    sh1, sc1, gt1, sh2, sc2, gt2 = mods(1, 0, b, False)
    q_p, qr_p, gates_p = _q_proj(x2, sh1, sc1, ng[1, 0], w_qt, qg, cos_p, sin_p, tm=tm)
    ncp = s // CMP_STRIDE
    x3 = _attn_prompt(q_p, qr_p, gates_p, x2, gt1, kcc, vcc, ksb, vsb, kwb, vwb,
                      _expand_matrix(s) * NEG_INF, _overlap_matrix(ncp, ncp - 1),
                      w_o.reshape(N_HEADS, HEAD_DIM, d))
    y_prompt = _ffn(x3, sh2, sc2, gt2, ng[1, 1], w_gu[1], w_down[1], tm=tm)

    xs = x_sample.reshape(1, bd, d)
    sh1, sc1, gt1, sh2, sc2, gt2 = mods(0, b, b + bd, True)
    xs1, v_s = _mixer_a(xs, sh1, sc1, gt1, ng[0, 0], w_in_a, vg, diag_s, bias_s, w_out_a,
                        single_token=True, tm=bd)
    xs2 = _ffn(xs1, sh2, sc2, gt2, ng[0, 1], w_gu[0], w_down[0], tm=bd)

    cos_s, sin_s = _rope_tables(jnp.full((bd,), past, dtype=jnp.int32))
    new_t = _kv_proj(xs2, kv_gain[None, :], w_kvt, kg1, kg2, cos_s, sin_s, tm=bd)[:6]
    sh1, sc1, gt1, sh2, sc2, gt2 = mods(1, b, b + bd, True)
    q_s, qr_s, gates_s = _q_proj(xs2, sh1, sc1, ng[1, 0], w_qt, qg, cos_s, sin_s, tm=bd)

    def per_head(t):
        r = t[0].T.reshape(bd, KV_HEADS, 1, HEAD_DIM)
        return jnp.broadcast_to(r, (bd, KV_HEADS, HEADS_PER_GROUP, HEAD_DIM)).reshape(bd, N_HEADS, HEAD_DIM)

    def lane_pad(t):
        return jnp.pad(t.reshape(bd, N_HEADS, HEAD_DIM), ((0, 0), (0, 0), (0, LANES - HEAD_DIM)))

    caches = [c.transpose(0, 2, 3, 1).reshape(n_phys, KV_WIDTH, page)
              for c in (cache_cmp_k, cache_cmp_v, cache_slc_k, cache_slc_v)]
    win_k = state_win_k.transpose(0, 2, 3, 1).reshape(bd, KV_WIDTH, win)
    win_v = state_win_v.transpose(0, 2, 3, 1).reshape(bd, KV_WIDTH, win)
    new16 = [per_head(new_t[i]) for i in (2, 3, 4, 5)]
    new_cols = [new_t[i][0].T[:, :, None] for i in (4, 5)]
    gates16 = gates_s[0, :, :3 * N_HEADS].reshape(bd, N_HEADS, 3)
    o_s, wk_new, wv_new = _attn_sample(
        page_table, caches, win_k, win_v, lane_pad(q_s), lane_pad(qr_s), gates16, new16, new_cols,
        cw_k, cw_v, _expand_matrix(past), _overlap_matrix(past // CMP_STRIDE, past // CMP_STRIDE - 1))
    xs3 = _out_proj(o_s.reshape(1, bd, N_HEADS * HEAD_DIM), xs2, gt1, w_o)
    y_sample = _ffn(xs3, sh2, sc2, gt2, ng[1, 1], w_gu[1], w_down[1], tm=bd)

    keep = min(WINDOW, s)
    new_rows = [t[0].T.reshape(bd, 1, KV_HEADS, HEAD_DIM) for t in new_t[:4]]
    return (y_prompt, y_sample.reshape(bd, 1, d), v_p[None], v_s.reshape(1, bd, 1, width),
            _to_rows(kct_p), _to_rows(vct_p), _to_rows(kst_p), _to_rows(vst_p),
            _to_rows(kwt_p[:, :, s - keep:]), _to_rows(vwt_p[:, :, s - keep:]),
            new_rows[0], new_rows[1], new_rows[2], new_rows[3],
            _to_rows(wk_new), _to_rows(wv_new))
```

```python
import functools

import numpy as np
import jax
import jax.numpy as jnp
from jax import lax
from jax.experimental import pallas as pl
from jax.experimental.pallas import tpu as pltpu

F32 = jnp.float32
BF16 = jnp.bfloat16

EPS = 1e-6
NEG_INF = -1e30
FORCE_SCORE = 1e9
TINY = 1e-30
ROPE_THETA = 10000.0
LOG2E = 1.4426950408889634
KNOCKED_OUT = -3e38

N_HEADS = 16
HEAD_DIM = 64
KV_HEADS = 4
HEADS_PER_GROUP = N_HEADS // KV_HEADS
KV_WIDTH = KV_HEADS * HEAD_DIM
V_AUG = HEAD_DIM + 16
CHUNK = 128
A_GROUPS = 8
CMP_STRIDE = 16
CMP_BLOCK = 2 * CMP_STRIDE
SLC_BLOCK = 64
N_SELECT = 16
WINDOW = 512
Q_BLOCK = 128
LANES = 128
GATE_PAD = 128
VMEM_LIMIT = 56 * 1024 * 1024

_NT = (((1,), (1,)), ((), ()))


def _params(n_grid):
    return pltpu.CompilerParams(dimension_semantics=("arbitrary",) * n_grid, vmem_limit_bytes=VMEM_LIMIT)


def _dot(a, b):
    return jnp.dot(a, b, preferred_element_type=F32)


def _dot_nt(a, b):
    return lax.dot_general(a, b, _NT, preferred_element_type=F32)


def _rms(x, g):
    return x * lax.rsqrt(jnp.mean(x * x, axis=-1, keepdims=True) + EPS) * g


def _split_dot(x, w):
    hi = x.astype(BF16)
    lo = (x - hi.astype(F32)).astype(BF16)
    return _dot(hi, w) + _dot(lo, w)


def _iota(shape, dim):
    return lax.broadcasted_iota(jnp.int32, shape, dim)


def _const_spec(shape):
    nd = len(shape)
    return pl.BlockSpec(shape, lambda *_: (0,) * nd)


def _mod_spec(m, tm):
    d = m.shape[-1]
    if m.shape[1] == 1:
        return pl.BlockSpec((1, 1, d), lambda b, j: (b, 0, 0))
    return pl.BlockSpec((1, tm, d), lambda b, j: (b, j, 0))


def _ada_kernel(c_ref, w_ref, b_ref, o_ref):
    c = c_ref[...]
    s = (c * jax.nn.sigmoid(c)).astype(BF16)
    o_ref[0] = _dot(s, w_ref[0].astype(BF16)) + b_ref[0]


def _ada(c_all, w_ada, b_ada, tn=1536):
    n_layers, d, d6 = w_ada.shape
    m = c_all.shape[0]
    return pl.pallas_call(
        _ada_kernel,
        grid=(n_layers, d6 // tn),
        in_specs=[pl.BlockSpec((m, d), lambda l, j: (0, 0)),
                  pl.BlockSpec((1, d, tn), lambda l, j: (l, 0, j)),
                  pl.BlockSpec((1, 1, tn), lambda l, j: (l, 0, j))],
        out_specs=pl.BlockSpec((1, m, tn), lambda l, j: (l, 0, j)),
        out_shape=jax.ShapeDtypeStruct((n_layers, m, d6), F32),
        compiler_params=_params(2),
        name="ada_modulation",
    )(c_all, w_ada, b_ada.reshape(n_layers, 1, d6))


def _mixer_a_kernel(x_ref, sh_ref, sc_ref, gt_ref, ng_ref, win_ref, vg_ref, ws_ref, bias_ref, wout_ref,
                    x1_ref, v_ref, um_ref, *, single_token):
    x = x_ref[0]
    h = _rms(x, ng_ref[...]) * (1.0 + sc_ref[0]) + sh_ref[0]
    act = jax.nn.gelu(_dot(h.astype(BF16), win_ref[...]))
    width = act.shape[1] // 2
    u = act[:, :width]
    v = _rms(act[:, width:], vg_ref[...])
    if single_token:
        v_ref[0] = v
        um_ref[...] = (u * (v * ws_ref[...] + bias_ref[...])).astype(BF16)
    else:
        tm = x.shape[0]
        gd = width // A_GROUPS

        @pl.when(pl.program_id(1) == pl.num_programs(1) - 1)
        def _():
            v_ref[0] = v[tm - CHUNK:, :]

        causal = _iota((CHUNK, CHUNK), 0) >= _iota((CHUNK, CHUNK), 1)
        for g in range(A_GROUPS):
            wg = jnp.where(causal, ws_ref[g], 0.0).astype(BF16)
            cs = slice(g * gd, (g + 1) * gd)
            chunks = [slice(c * CHUNK, (c + 1) * CHUNK) for c in range(tm // CHUNK)]
            mixed = _dot(wg, jnp.concatenate([v[rs, cs] for rs in chunks], axis=1).astype(BF16))
            for c, rs in enumerate(chunks):
                um_ref[rs, cs] = (u[rs, cs] * (mixed[:, c * gd:(c + 1) * gd] + bias_ref[:, cs])).astype(BF16)
    x1_ref[0] = x + gt_ref[0] * _dot(um_ref[...], wout_ref[...])


def _mixer_a(x, sh, sc, gt, ng, w_in, v_gain, ws, bias, w_out, *, single_token, tm):
    bm, sm, d = x.shape
    width = w_out.shape[0]
    v_rows = sm if single_token else CHUNK
    kern = functools.partial(_mixer_a_kernel, single_token=single_token)
    return pl.pallas_call(
        kern,
        grid=(bm, sm // tm),
        in_specs=[pl.BlockSpec((1, tm, d), lambda b, j: (b, j, 0)),
                  _mod_spec(sh, tm), _mod_spec(sc, tm), _mod_spec(gt, tm),
                  _const_spec(ng.shape), _const_spec(w_in.shape), _const_spec(v_gain.shape),
                  _const_spec(ws.shape), _const_spec(bias.shape), _const_spec(w_out.shape)],
        out_specs=[pl.BlockSpec((1, tm, d), lambda b, j: (b, j, 0)),
                   pl.BlockSpec((1, v_rows, width), lambda b, j: (b, 0, 0))],
        out_shape=[jax.ShapeDtypeStruct((bm, sm, d), F32),
                   jax.ShapeDtypeStruct((bm, v_rows, width), F32)],
        scratch_shapes=[pltpu.VMEM((tm, width), BF16)],
        compiler_params=_params(2),
        name="mixer_a",
    )(x, sh, sc, gt, ng, w_in, v_gain, ws, bias, w_out)


def _ffn_kernel(x_ref, sh_ref, sc_ref, gt_ref, ng_ref, wgu_ref, wd_ref, o_ref, act_ref, *, fc):
    x = x_ref[0]
    h = (_rms(x, ng_ref[...]) * (1.0 + sc_ref[0]) + sh_ref[0]).astype(BF16)
    dff = wd_ref.shape[0]
    for j in range(dff // fc):
        g = _dot(h, wgu_ref[:, j * fc:(j + 1) * fc])
        u = _dot(h, wgu_ref[:, dff + j * fc:dff + (j + 1) * fc])
        act_ref[:, j * fc:(j + 1) * fc] = (g * jax.nn.sigmoid(g) * u).astype(BF16)
    o_ref[0] = x + gt_ref[0] * _dot(act_ref[...], wd_ref[...])


def _ffn(x, sh, sc, gt, ng, w_gu, w_down, *, tm, fc=256):
    bm, sm, d = x.shape
    dff = w_down.shape[0]
    assert dff % fc == 0
    return pl.pallas_call(
        functools.partial(_ffn_kernel, fc=fc),
        grid=(bm, sm // tm),
        in_specs=[pl.BlockSpec((1, tm, d), lambda b, j: (b, j, 0)),
                  _mod_spec(sh, tm), _mod_spec(sc, tm), _mod_spec(gt, tm),
                  _const_spec(ng.shape), _const_spec(w_gu.shape), _const_spec(w_down.shape)],
        out_specs=pl.BlockSpec((1, tm, d), lambda b, j: (b, j, 0)),
        out_shape=jax.ShapeDtypeStruct((bm, sm, d), F32),
        scratch_shapes=[pltpu.VMEM((tm, dff), BF16)],
        compiler_params=_params(2),
        name="swiglu_ffn",
    )(x, sh, sc, gt, ng, w_gu, w_down)


def _head_norm_rope(t, gain_col, cos, sin):
    rows, n = t.shape
    t3 = t.reshape(rows // HEAD_DIM, HEAD_DIM, n)
    y = t3 * lax.rsqrt(jnp.mean(t3 * t3, axis=1, keepdims=True) + EPS) * gain_col[None]
    half = HEAD_DIM // 2
    y1 = y[:, :half, :]
    y2 = y[:, half:, :]
    c = cos[None]
    s = sin[None]
    rot = jnp.concatenate([y1 * c - y2 * s, y2 * c + y1 * s], axis=1)
    return y.reshape(rows, n), rot.reshape(rows, n)


def _kv_proj_kernel(x_ref, kvg_ref, w_ref, g1_ref, g2_ref, cos_ref, sin_ref,
                    kc_ref, vc_ref, ks_ref, vs_ref, kw_ref, vw_ref, ksb_ref, vsb_ref, kwb_ref, vwb_ref):
    xn = _rms(x_ref[0], kvg_ref[...]).astype(BF16)
    kvt = _dot_nt(w_ref[...], xn)
    cos = cos_ref[...]
    sin = sin_ref[...]

    def part(i):
        return kvt[i * KV_WIDTH:(i + 1) * KV_WIDTH]

    kc_ref[0] = part(0)
    vc_ref[0] = part(1)
    _, ks = _head_norm_rope(part(2), g1_ref[...], cos, sin)
    _, kw = _head_norm_rope(part(4), g2_ref[...], cos, sin)
    vs = part(3)
    vw = part(5)
    ks_ref[0] = ks
    vs_ref[0] = vs
    @pl.when(pl.program_id(1) == pl.num_programs(1) - 1)
    def _():
        kw_ref[0] = kw
        vw_ref[0] = vw
    ksb_ref[0] = ks.astype(BF16)
    kwb_ref[0] = kw.astype(BF16)
    ones = jnp.ones((V_AUG - HEAD_DIM, vs.shape[1]), BF16)
    for g in range(KV_HEADS):
        hs = slice(g * HEAD_DIM, (g + 1) * HEAD_DIM)
        vsb_ref[0, g, :HEAD_DIM] = vs[hs].astype(BF16)
        vsb_ref[0, g, HEAD_DIM:] = ones
        vwb_ref[0, g, :HEAD_DIM] = vw[hs].astype(BF16)
        vwb_ref[0, g, HEAD_DIM:] = ones


def _kv_proj(x, kv_gain, w_kvt, g1, g2, cos_t, sin_t, *, tm):
    bm, sm, d = x.shape
    half = HEAD_DIM // 2
    out_spec = pl.BlockSpec((1, KV_WIDTH, tm), lambda b, j: (b, 0, j))
    aug_spec = pl.BlockSpec((1, KV_HEADS, V_AUG, tm), lambda b, j: (b, 0, 0, j))
    last_spec = pl.BlockSpec((1, KV_WIDTH, tm), lambda b, j: (b, 0, 0))
    k_bf = jax.ShapeDtypeStruct((bm, KV_WIDTH, sm), BF16)
    v_bf = jax.ShapeDtypeStruct((bm, KV_HEADS, V_AUG, sm), BF16)
    return pl.pallas_call(
        _kv_proj_kernel,
        grid=(bm, sm // tm),
        in_specs=[pl.BlockSpec((1, tm, d), lambda b, j: (b, j, 0)),
                  _const_spec(kv_gain.shape), _const_spec(w_kvt.shape), _const_spec(g1.shape),
                  _const_spec(g2.shape),
                  pl.BlockSpec((half, tm), lambda b, j: (0, j)),
                  pl.BlockSpec((half, tm), lambda b, j: (0, j))],
        out_specs=[out_spec] * 4 + [last_spec] * 2 + [out_spec, aug_spec, out_spec, aug_spec],
        out_shape=[jax.ShapeDtypeStruct((bm, KV_WIDTH, sm), F32)] * 4
                  + [jax.ShapeDtypeStruct((bm, KV_WIDTH, tm), F32)] * 2 + [k_bf, v_bf, k_bf, v_bf],
        compiler_params=_params(2),
        name="kv_projection",
    )(x, kv_gain, w_kvt, g1, g2, cos_t, sin_t)


def _q_proj_kernel(x_ref, sh_ref, sc_ref, ng_ref, w_ref, qg_ref, cos_ref, sin_ref, q_ref, qr_ref, gates_ref):
    h = (_rms(x_ref[0], ng_ref[...]) * (1.0 + sc_ref[0]) + sh_ref[0]).astype(BF16)
    pt = _dot_nt(w_ref[...], h)
    nq = N_HEADS * HEAD_DIM
    qn, qrot = _head_norm_rope(pt[:nq], qg_ref[...], cos_ref[...], sin_ref[...])
    scale = LOG2E * HEAD_DIM ** -0.5
    q_ref[0] = (qn * scale).T.astype(BF16)
    qr_ref[0] = (qrot * scale).T.astype(BF16)
    gates_ref[0] = jax.nn.sigmoid(pt[nq:]).T


def _q_proj(x, sh, sc, ng, w_int, q_gain, cos_t, sin_t, *, tm):
    bm, sm, d = x.shape
    nq = N_HEADS * HEAD_DIM
    half = HEAD_DIM // 2
    return pl.pallas_call(
        _q_proj_kernel,
        grid=(bm, sm // tm),
        in_specs=[pl.BlockSpec((1, tm, d), lambda b, j: (b, j, 0)),
                  _mod_spec(sh, tm), _mod_spec(sc, tm),
                  _const_spec(ng.shape), _const_spec(w_int.shape), _const_spec(q_gain.shape),
                  pl.BlockSpec((half, tm), lambda b, j: (0, j)),
                  pl.BlockSpec((half, tm), lambda b, j: (0, j))],
        out_specs=[pl.BlockSpec((1, tm, nq), lambda b, j: (b, j, 0)),
                   pl.BlockSpec((1, tm, nq), lambda b, j: (b, j, 0)),
                   pl.BlockSpec((1, tm, GATE_PAD), lambda b, j: (b, j, 0))],
        out_shape=[jax.ShapeDtypeStruct((bm, sm, nq), BF16),
                   jax.ShapeDtypeStruct((bm, sm, nq), BF16),
                   jax.ShapeDtypeStruct((bm, sm, GATE_PAD), F32)],
        compiler_params=_params(2),
        name="q_projection",
    )(x, sh, sc, ng, w_int, q_gain, cos_t, sin_t)


def _store_rows(xs_ref, xt):
    for c in range(xs_ref.shape[0]):
        xs_ref[c] = xt[c * LANES:(c + 1) * LANES, :].T


def _compress_halves(xs_ref, n, pe_a, pe_b, w1a_ref, w1b_ref):
    low = _iota((n, LANES), 1) < HEAD_DIM
    per_head = [[] for _ in range(KV_HEADS)]
    for c in range(KV_WIDTH // LANES):
        for r in range(0, CMP_STRIDE, 2):
            x0 = xs_ref[c, pl.ds(r, n, stride=CMP_STRIDE), :]
            x1 = xs_ref[c, pl.ds(r + 1, n, stride=CMP_STRIDE), :]
            per_head[2 * c].append(jnp.where(low, x0, pltpu.roll(x1, shift=HEAD_DIM, axis=1)))
            per_head[2 * c + 1].append(jnp.where(low, pltpu.roll(x0, shift=HEAD_DIM, axis=1), x1))
    lhs = jnp.concatenate([jnp.concatenate(p, axis=1) for p in per_head], axis=0)
    a = _dot((lhs + pe_a).astype(BF16), w1a_ref[...])
    b = _dot((lhs + pe_b).astype(BF16), w1b_ref[...])
    return a, b


def _compress_finish(a, b_next, b1, w2_ref, b2, gain):
    hid = jax.nn.gelu(a + b_next + b1)
    out = _dot(hid.astype(BF16), w2_ref[...]) + b2
    if gain is not None:
        ms = jnp.sum(out * out, axis=-1, keepdims=True) * (1.0 / HEAD_DIM)
        out = out * lax.rsqrt(ms + EPS) * gain
    return out


def _compress_kernel(x_ref, nxt_ref, pea_ref, peb_ref, w1a_ref, w1b_ref, b1_ref, w2_ref, b2_ref, gain_ref,
                     o_ref, xs_ref, xn_ref, *, norm):
    lc = x_ref.shape[-1]
    n = lc // CMP_STRIDE
    nn = nxt_ref.shape[-1] // CMP_STRIDE
    _store_rows(xs_ref, x_ref[0])
    _store_rows(xn_ref, nxt_ref[0])
    a, b = _compress_halves(xs_ref, n, pea_ref[...], peb_ref[...], w1a_ref, w1b_ref)
    _, bn = _compress_halves(xn_ref, nn, pea_ref[...], peb_ref[...], w1a_ref, w1b_ref)
    last_row = _iota((n, 1), 0) == n - 1
    b_next = jnp.concatenate(
        [jnp.where(last_row, bn[g * nn:g * nn + 1, :], pltpu.roll(b[g * n:(g + 1) * n], shift=n - 1, axis=0))
         for g in range(KV_HEADS)], axis=0)
    out = _compress_finish(a, b_next, b1_ref[...], w2_ref, b2_ref[...], gain_ref[...] if norm else None)
    for g in range(KV_HEADS):
        o_ref[0, g] = out[g * n:(g + 1) * n].T[:HEAD_DIM]


def _compress_prompt(xt, cw, *, norm, lc=2048, ln=256):
    b, _, s = xt.shape
    lc = min(lc, s)
    n = lc // CMP_STRIDE
    n_next = s // ln
    return pl.pallas_call(
        functools.partial(_compress_kernel, norm=norm),
        grid=(b, s // lc),
        in_specs=[pl.BlockSpec((1, KV_WIDTH, lc), lambda i, c: (i, 0, c)),
                  pl.BlockSpec((1, KV_WIDTH, ln),
                               lambda i, c: (i, 0, jnp.minimum((c + 1) * (lc // ln), n_next - 1)))]
                 + [_const_spec(w.shape) for w in cw],
        out_specs=pl.BlockSpec((1, KV_HEADS, HEAD_DIM, n), lambda i, c: (i, 0, 0, c)),
        out_shape=jax.ShapeDtypeStruct((b, KV_HEADS, HEAD_DIM, s // CMP_STRIDE), F32),
        scratch_shapes=[pltpu.VMEM((KV_WIDTH // LANES, lc, LANES), F32),
                        pltpu.VMEM((KV_WIDTH // LANES, ln, LANES), F32)],
        compiler_params=_params(2),
        name="compress_prompt",
    )(xt, xt, *cw)


def _select_blocks(scores_t, n_sel):
    srow = _iota(scores_t[0].shape, 0).astype(F32)
    n_blocks = float(scores_t[0].shape[0])
    vals = list(scores_t)
    sels = [jnp.zeros(v.shape, F32) for v in vals]
    for _ in range(n_sel):
        for g, v in enumerate(vals):
            best = jnp.max(v, axis=0, keepdims=True)
            idx = jnp.min(jnp.where(v == best, srow, n_blocks), axis=0, keepdims=True)
            hit = srow == idx
            vals[g] = jnp.where(hit, KNOCKED_OUT, v)
            sels[g] = jnp.where(hit, 1.0, sels[g])
    return sels


def _attn_prompt_kernel(q_ref, qr_ref, gates_ref, x_ref, g1_ref, kct_ref, vct_ref, ks_ref, vs_ref, kw_ref,
                        vw_ref, nege_ref, ov_ref, wo_ref, o_ref, *, nc, n_sel, tk):
    n = pl.program_id(1)
    qb = Q_BLOCK
    hq = HEADS_PER_GROUP * qb
    q0 = n * qb
    ncp = kct_ref.shape[-1]
    qpos = q0 + _iota((qb, 1), 0)
    cur = qpos // SLC_BLOCK
    gates = gates_ref[0]
    q_all = q_ref[0]
    qr_all = qr_ref[0]

    ci = _iota((qb, ncp), 1)
    vis = (ci < nc) & (ci * CMP_STRIDE + (CMP_BLOCK - 1) <= qpos)
    vis_bias = jnp.where(vis, 0.0, NEG_INF)
    vis_one = jnp.where(vis, 1.0, 0.0)
    blk = _iota((qb, LANES), 1)
    valid = blk <= cur
    forced = (blk == 0) | (blk == cur) | (blk == cur - 1)
    n_tiles = (q0 + qb + tk - 1) // tk

    def group_rows(x_all, g):
        return jnp.concatenate([x_all[:, h * HEAD_DIM:(h + 1) * HEAD_DIM]
                                for h in range(g * HEADS_PER_GROUP, (g + 1) * HEADS_PER_GROUP)], axis=0)

    ocs, imps = [], []
    for g in range(KV_HEADS):
        s = _dot(group_rows(q_all, g), kct_ref[0, g].astype(BF16)).reshape(HEADS_PER_GROUP, qb, ncp)
        s = s + vis_bias[None]
        p = jnp.exp2(s - jnp.max(s, axis=-1, keepdims=True)) * vis_one[None]
        pc = p * (1.0 / jnp.maximum(jnp.sum(p, axis=-1, keepdims=True), TINY))
        ocs.append(_dot_nt(pc.reshape(hq, ncp).astype(BF16), vct_ref[0, g].astype(BF16)))
        imps.append(_split_dot(pc[0] + pc[1] + pc[2] + pc[3], ov_ref[...]))

    k_last = (n_tiles - 1) * tk
    causal_bias = jnp.where(k_last + _iota((qb, tk), 1) <= qpos, 0.0, NEG_INF)
    w_tiles = WINDOW // qb + 1
    w_offs, w_biases = [], []
    for t in range(w_tiles):
        off = q0 - WINDOW + qb * t
        w_offs.append(pl.multiple_of(jnp.maximum(off, 0), qb))
        wpos = off + _iota((qb, qb), 1)
        dpos = qpos - wpos
        w_biases.append(jnp.where((dpos >= 0) & (dpos <= WINDOW) & (wpos >= 0), 0.0, NEG_INF))
    w_bias = jnp.concatenate(w_biases, axis=1)
    wk = w_tiles * qb

    qrgs = [group_rows(qr_all, g) for g in range(KV_HEADS)]
    acc_ws = []
    for g in range(KV_HEADS):
        hs = slice(g * HEAD_DIM, (g + 1) * HEAD_DIM)
        kwt = jnp.concatenate([kw_ref[0, hs, pl.ds(o, qb)] for o in w_offs], axis=1)
        vwt = jnp.concatenate([vw_ref[0, g, :, pl.ds(o, qb)] for o in w_offs], axis=1)
        sw = (_dot(qrgs[g], kwt).reshape(HEADS_PER_GROUP, qb, wk) + w_bias[None]).reshape(hq, wk)
        pw = jnp.exp2(sw - jnp.max(sw, axis=-1, keepdims=True)).astype(BF16)
        acc_ws.append(_dot_nt(pw, vwt))

    scores = [jnp.where(valid, jnp.where(forced, KNOCKED_OUT, imp), NEG_INF).T for imp in imps]
    sel_t = _select_blocks(scores, n_sel - 3)
    forced_one = jnp.where(forced, 1.0, 0.0)

    lhss = []
    for g in range(KV_HEADS):
        not_sel = jnp.where(valid, 1.0 - jnp.maximum(sel_t[g].T, forced_one), 1.0).astype(BF16)
        lhss.append(jnp.concatenate([jnp.concatenate([not_sel] * HEADS_PER_GROUP, axis=0), qrgs[g]], axis=1))

    def tile(k0, width, carry, causal):
        neg = nege_ref[:, pl.ds(k0, width)]
        out = []
        for g in range(KV_HEADS):
            m_i, acc = carry[g]
            rhs = jnp.concatenate([neg, ks_ref[0, g * HEAD_DIM:(g + 1) * HEAD_DIM, pl.ds(k0, width)]], axis=0)
            st = _dot(lhss[g], rhs)
            if causal:
                st = (st.reshape(HEADS_PER_GROUP, qb, width) + causal_bias[None]).reshape(hq, width)
            m_new = jnp.maximum(m_i, jnp.max(st, axis=-1, keepdims=True))
            pt = jnp.exp2(st - m_new).astype(BF16)
            acc = jnp.exp2(m_i - m_new) * acc + _dot_nt(pt, vs_ref[0, g, :, pl.ds(k0, width)])
            out.append((m_new, acc))
        return tuple(out)

    n_full = n_tiles - 1
    one = n_full % 2
    two = (n_full // 2) % 2
    init = tuple((jnp.full((hq, 1), NEG_INF, F32), jnp.zeros((hq, V_AUG), F32)) for _ in range(KV_HEADS))
    carry = lax.cond(one == 1, lambda c: tile(0, tk, c, False), lambda c: c, init)
    carry = lax.cond(two == 1, lambda c: tile(pl.multiple_of(one * tk, tk), 2 * tk, c, False), lambda c: c, carry)
    carry = lax.fori_loop(
        0, n_full // 4,
        lambda j, c: tile(pl.multiple_of((one + 2 * two + 4 * j) * tk, tk), 4 * tk, c, False), carry)
    accs = [c[1] for c in tile(pl.multiple_of(n_full * tk, tk), tk, carry, True)]

    proj = jnp.zeros((qb, wo_ref.shape[-1]), F32)
    for g in range(KV_HEADS):
        acc_s = accs[g]
        acc_w = acc_ws[g]
        for i in range(HEADS_PER_GROUP):
            h = g * HEADS_PER_GROUP + i
            rs = slice(i * qb, (i + 1) * qb)
            gs = gates[:, 3 * h + 1:3 * h + 2] / jnp.maximum(acc_s[rs, HEAD_DIM:HEAD_DIM + 1], TINY)
            gw = gates[:, 3 * h + 2:3 * h + 3] / jnp.maximum(acc_w[rs, HEAD_DIM:HEAD_DIM + 1], TINY)
            o_h = (ocs[g][rs] * gates[:, 3 * h:3 * h + 1] + acc_s[rs, :HEAD_DIM] * gs
                   + acc_w[rs, :HEAD_DIM] * gw)
            proj = proj + _dot(o_h.astype(BF16), wo_ref[h])
    o_ref[0] = x_ref[0] + g1_ref[0] * proj


def _attn_prompt(q, qr, gates, x, gate1, kct, vct, ksb, vsb, kwb, vwb, nege, ov, wo, *, tk=512):
    b, s, d = x.shape
    nq = q.shape[-1]
    ncp = kct.shape[-1]
    nc = s // CMP_STRIDE - 1
    ns = -(-s // SLC_BLOCK)
    tk = min(tk, s)
    kern = functools.partial(_attn_prompt_kernel, nc=nc, n_sel=min(N_SELECT, ns), tk=tk)
    full_k = pl.BlockSpec((1, KV_WIDTH, s), lambda i, j: (i, 0, 0), pipeline_mode=pl.Buffered(1))
    full_v = pl.BlockSpec((1, KV_HEADS, V_AUG, s), lambda i, j: (i, 0, 0, 0), pipeline_mode=pl.Buffered(1))
    cmp_spec = pl.BlockSpec((1, KV_HEADS, HEAD_DIM, ncp), lambda i, j: (i, 0, 0, 0))
    return pl.pallas_call(
        kern,
        grid=(b, s // Q_BLOCK),
        in_specs=[pl.BlockSpec((1, Q_BLOCK, nq), lambda i, j: (i, j, 0)),
                  pl.BlockSpec((1, Q_BLOCK, nq), lambda i, j: (i, j, 0)),
                  pl.BlockSpec((1, Q_BLOCK, GATE_PAD), lambda i, j: (i, j, 0)),
                  pl.BlockSpec((1, Q_BLOCK, d), lambda i, j: (i, j, 0)),
                  pl.BlockSpec((1, 1, d), lambda i, j: (i, 0, 0)),
                  cmp_spec, cmp_spec, full_k, full_v, full_k, full_v,
                  pl.BlockSpec(nege.shape, lambda i, j: (0, 0), pipeline_mode=pl.Buffered(1)),
                  _const_spec(ov.shape), _const_spec(wo.shape)],
        out_specs=pl.BlockSpec((1, Q_BLOCK, d), lambda i, j: (i, j, 0)),
        out_shape=jax.ShapeDtypeStruct((b, s, d), F32),
        compiler_params=_params(2),
        name="nsa_prompt",
    )(q, qr, gates, x, gate1, kct, vct, ksb, vsb, kwb, vwb, nege, ov, wo)


def _attn_sample_kernel(pt_ref, ck_ref, cv_ref, sk_ref, sv_ref, wk_ref, wv_ref, q_ref, qr_ref, gates_ref,
                        ksn_ref, vsn_ref, kwn_ref, vwn_ref, kwc_ref, vwc_ref,
                        kpea_ref, kpeb_ref, kw1a_ref, kw1b_ref, kb1_ref, kw2_ref, kb2_ref, kgain_ref,
                        vpea_ref, vpeb_ref, vw1a_ref, vw1b_ref, vb1_ref, vw2_ref, vb2_ref, vgain_ref,
                        e_ref, ov_ref,
                        o_ref, wko_ref, wvo_ref,
                        slabs, xs_ref, sems, *, past, page, n_sel):
    del vgain_ref
    b = pl.program_id(0)
    slot = b % 2
    caches = (ck_ref, cv_ref, sk_ref, sv_ref)

    def page_copy(seq, slot_, t, p):
        return pltpu.make_async_copy(caches[t].at[pt_ref[seq, p]],
                                     slabs.at[slot_, t, :, pl.ds(p * page, page)], sems.at[slot_, t])

    def gather(seq, slot_, start):
        for t in range(len(caches)):
            for p in range(past // page):
                cp = page_copy(seq, slot_, t, p)
                cp.start() if start else cp.wait()

    @pl.when(b == 0)
    def _():
        gather(0, 0, True)

    @pl.when(b + 1 < pl.num_programs(0))
    def _():
        gather(b + 1, 1 - slot, True)

    gather(b, slot, False)
    sl_ck, sl_cv, sl_sk, sl_sv = (slabs.at[slot, t] for t in range(len(caches)))

    def compute():
        nh = N_HEADS
        qpos = past
        n = past // CMP_STRIDE
        nc = n - 1
        cur = qpos // SLC_BLOCK
        win = wk_ref.shape[-1]
        hrow = _iota((nh, 1), 0) // HEADS_PER_GROUP

        def by_group(fn, width):
            out = jnp.zeros((nh, width), F32)
            for g in range(KV_HEADS):
                out = jnp.where(hrow == g, fn(g), out)
            return out

        def compress(slab, pea, peb, w1a, w1b, b1, w2, b2, gain):
            _store_rows(xs_ref, slab[...])
            a, bh = _compress_halves(xs_ref, n, pea[...], peb[...], w1a, w1b)
            b_next = jnp.concatenate([pltpu.roll(bh[g * n:(g + 1) * n], shift=n - 1, axis=0)
                                      for g in range(KV_HEADS)], axis=0)
            out = _compress_finish(a, b_next, b1[...], w2, b2[...], gain)
            return [out[g * n:(g + 1) * n] for g in range(KV_HEADS)]

        kc = compress(sl_ck, kpea_ref, kpeb_ref, kw1a_ref, kw1b_ref, kb1_ref, kw2_ref, kb2_ref, kgain_ref[...])
        vc = compress(sl_cv, vpea_ref, vpeb_ref, vw1a_ref, vw1b_ref, vb1_ref, vw2_ref, vb2_ref, None)

        q = q_ref[0]
        qr = qr_ref[0]
        qr_lo = qr[:, :HEAD_DIM]
        qr_f = qr_lo.astype(F32)

        sc = by_group(lambda g: _dot_nt(q, kc[g].astype(BF16)), n)
        ci = _iota((nh, n), 1)
        vis = (ci < nc) & (ci * CMP_STRIDE + (CMP_BLOCK - 1) <= qpos)
        sc = jnp.where(vis, sc, NEG_INF)
        pc = jnp.where(vis, jnp.exp2(sc - jnp.max(sc, axis=-1, keepdims=True)), 0.0)
        pc = pc / jnp.maximum(jnp.sum(pc, axis=-1, keepdims=True), TINY)
        o_c = by_group(lambda g: _dot(pc.astype(BF16), vc[g].astype(BF16)), LANES)[:, :HEAD_DIM]

        grow = _iota((nh, 1), 0)
        psum = jnp.zeros((nh, n), F32)
        for g in range(KV_HEADS):
            pg = jnp.sum(pc[g * HEADS_PER_GROUP:(g + 1) * HEADS_PER_GROUP], axis=0, keepdims=True)
            psum = jnp.where(grow == g, pg, psum)
        imp = _split_dot(psum, ov_ref[...])
        blk = _iota((nh, LANES), 1)
        valid = blk <= cur
        forced = (blk == 0) | (blk == cur) | (blk == cur - 1)
        score = jnp.where(valid, jnp.where(forced, FORCE_SCORE, imp), NEG_INF)
        rank = jnp.zeros((nh, LANES), F32)
        for sp in range(cur + 1):
            col = score[:, sp:sp + 1]
            rank = rank + jnp.where(col > score, 1.0, jnp.where(col == score, jnp.where(sp < blk, 1.0, 0.0), 0.0))
        sel_g = jnp.where(valid, jnp.where(rank < n_sel, 1.0, 0.0), 0.0)
        sel = by_group(lambda g: sel_g[g:g + 1, :], LANES)

        chosen = _dot(sel.astype(BF16), e_ref[...])
        kpos = _iota((nh, past), 1)
        bias = jnp.where(kpos <= qpos, jnp.where(chosen > 0.5, 0.0, NEG_INF), NEG_INF)
        ss = by_group(lambda g: _dot(qr_lo, sl_sk[g * HEAD_DIM:(g + 1) * HEAD_DIM, :].astype(BF16)), past) + bias
        s_new = jnp.sum(qr_f * ksn_ref[0], axis=-1, keepdims=True)
        s_new = jnp.where(sel[:, cur:cur + 1] > 0.5, s_new, NEG_INF)
        m = jnp.maximum(jnp.max(ss, axis=-1, keepdims=True), s_new)
        ps = jnp.exp2(ss - m)
        p_new = jnp.exp2(s_new - m)
        den = jnp.maximum(jnp.sum(ps, axis=-1, keepdims=True) + p_new, TINY)
        psb = ps.astype(BF16)
        o_s = by_group(lambda g: _dot_nt(psb, sl_sv[g * HEAD_DIM:(g + 1) * HEAD_DIM, :].astype(BF16)), HEAD_DIM)
        o_s = (o_s + p_new * vsn_ref[0]) / den

        wpos = qpos - win + _iota((nh, win), 1)
        dpos = qpos - wpos
        okw = (dpos >= 0) & (dpos <= WINDOW) & (wpos >= 0)
        sw = by_group(lambda g: _dot(qr_lo, wk_ref[0, g * HEAD_DIM:(g + 1) * HEAD_DIM, :].astype(BF16)), win)
        sw = sw + jnp.where(okw, 0.0, NEG_INF)
        sw_new = jnp.sum(qr_f * kwn_ref[0], axis=-1, keepdims=True)
        mw = jnp.maximum(jnp.max(sw, axis=-1, keepdims=True), sw_new)
        pw = jnp.exp2(sw - mw)
        pw_new = jnp.exp2(sw_new - mw)
        denw = jnp.maximum(jnp.sum(pw, axis=-1, keepdims=True) + pw_new, TINY)
        pwb = pw.astype(BF16)
        o_w = by_group(lambda g: _dot_nt(pwb, wv_ref[0, g * HEAD_DIM:(g + 1) * HEAD_DIM, :].astype(BF16)), HEAD_DIM)
        o_w = (o_w + pw_new * vwn_ref[0]) / denw

        gates = gates_ref[0]
        o_ref[0] = o_c * gates[:, 0:1] + o_s * gates[:, 1:2] + o_w * gates[:, 2:3]

        lane = _iota((KV_WIDTH, win), 1)
        wko_ref[0] = jnp.where(lane == win - 1, kwc_ref[0], pltpu.roll(wk_ref[0], shift=win - 1, axis=1))
        wvo_ref[0] = jnp.where(lane == win - 1, vwc_ref[0], pltpu.roll(wv_ref[0], shift=win - 1, axis=1))

    compute()


def _attn_sample(page_table, caches, win_k, win_v, q, qr, gates, new16, new_cols, cw_k, cw_v, e_mat, ov):
    bd, n_pages = page_table.shape
    page = caches[0].shape[-1]
    past = n_pages * page
    win = win_k.shape[-1]
    cur = past // SLC_BLOCK
    kern = functools.partial(_attn_sample_kernel, past=past, page=page, n_sel=min(N_SELECT, cur + 1))

    def per_seq(a):
        nd = a.ndim
        return pl.BlockSpec((1,) + a.shape[1:], lambda b, pt: (b,) + (0,) * (nd - 1))

    def const(a):
        nd = a.ndim
        return pl.BlockSpec(a.shape, lambda b, pt: (0,) * nd)

    consts = list(cw_k) + list(cw_v) + [e_mat, ov]
    grid_spec = pltpu.PrefetchScalarGridSpec(
        num_scalar_prefetch=1,
        grid=(bd,),
        in_specs=[pl.BlockSpec(memory_space=pl.ANY)] * len(caches)
                 + [per_seq(win_k), per_seq(win_v), per_seq(q), per_seq(qr), per_seq(gates)]
                 + [per_seq(a) for a in new16] + [per_seq(a) for a in new_cols] + [const(a) for a in consts],
        out_specs=[pl.BlockSpec((1, N_HEADS, HEAD_DIM), lambda b, pt: (b, 0, 0)),
                   pl.BlockSpec((1, KV_WIDTH, win), lambda b, pt: (b, 0, 0)),
                   pl.BlockSpec((1, KV_WIDTH, win), lambda b, pt: (b, 0, 0))],
        scratch_shapes=[pltpu.VMEM((2, len(caches), KV_WIDTH, past), F32),
                        pltpu.VMEM((KV_WIDTH // LANES, past, LANES), F32),
                        pltpu.SemaphoreType.DMA((2, len(caches)))],
    )
    return pl.pallas_call(
        kern,
        grid_spec=grid_spec,
        out_shape=[jax.ShapeDtypeStruct((bd, N_HEADS, HEAD_DIM), F32),
                   jax.ShapeDtypeStruct((bd, KV_WIDTH, win), F32),
                   jax.ShapeDtypeStruct((bd, KV_WIDTH, win), F32)],
        compiler_params=_params(1),
        name="nsa_decode",
    )(page_table, *caches, win_k, win_v, q, qr, gates, *new16, *new_cols, *consts)


def _out_proj_kernel(o_ref, x_ref, g_ref, w_ref, y_ref):
    y_ref[0] = x_ref[0] + g_ref[0] * _dot(o_ref[0].astype(BF16), w_ref[...])


def _out_proj(o, x, gate, w):
    bm, sm, d = x.shape
    spec = pl.BlockSpec((1, sm, d), lambda b: (b, 0, 0))
    return pl.pallas_call(
        _out_proj_kernel,
        grid=(bm,),
        in_specs=[pl.BlockSpec((1, sm, o.shape[-1]), lambda b: (b, 0, 0)), spec, spec, _const_spec(w.shape)],
        out_specs=spec,
        out_shape=jax.ShapeDtypeStruct((bm, sm, d), F32),
        compiler_params=_params(1),
        name="attn_out_proj",
    )(o, x, gate, w)


def _rope_tables(pos):
    half = HEAD_DIM // 2
    inv_freq = ROPE_THETA ** (-jnp.arange(half, dtype=F32) / half)
    ang = inv_freq[:, None] * pos.astype(F32)[None, :]
    return jnp.cos(ang), jnp.sin(ang)


def _overlap_matrix(n_rows, nc):
    i = np.arange(n_rows)[:, None] * CMP_STRIDE
    s = np.arange(LANES)[None, :] * SLC_BLOCK
    ov = (i < s + SLC_BLOCK) & (i + CMP_BLOCK > s) & (np.arange(n_rows)[:, None] < nc)
    return jnp.asarray(ov.astype(np.float32), dtype=BF16)


def _expand_matrix(n_keys):
    e = (np.arange(n_keys)[None, :] // SLC_BLOCK) == np.arange(LANES)[:, None]
    return jnp.asarray(e.astype(np.float32), dtype=BF16)


def _compress_weights(t, cmp_pe, cmp_w1, cmp_b1, cmp_w2, cmp_b2, gain):
    half = CMP_STRIDE * HEAD_DIM
    pad = LANES - HEAD_DIM
    return (cmp_pe[t, :CMP_STRIDE].reshape(1, half), cmp_pe[t, CMP_STRIDE:].reshape(1, half),
            cmp_w1[t, :half].astype(BF16), cmp_w1[t, half:].astype(BF16), cmp_b1[t][None, :],
            jnp.pad(cmp_w2[t], ((0, 0), (0, pad))).astype(BF16), jnp.pad(cmp_b2[t], (0, pad))[None, :],
            jnp.pad(gain, (0, pad))[None, :])


def _to_rows(t):
    b, _, n = t.shape
    return t.reshape(b, KV_HEADS, HEAD_DIM, n).transpose(0, 3, 1, 2)


def kernel(x_prompt, x_sample, cache_cmp_k, cache_cmp_v, cache_slc_k, cache_slc_v, state_win_k, state_win_v,
           page_table, c_prompt, c_sample, w_ada, b_ada, norm_g, a_w_in, a_v_gain, a_w_s, a_b_s, a_w_out,
           b_w_in, b_q_gain, b_w_out, kv_gain, w_kv, k_gains, cmp_pe, cmp_w1, cmp_b1, cmp_w2, cmp_b2,
           ffn_w_gu, ffn_w_down):
    b, s, d = x_prompt.shape
    bd = x_sample.shape[0]
    assert x_sample.shape[1] == 1 and w_ada.shape[0] == 2 and a_w_in.shape[0] == 1 and b_w_in.shape[0] == 1
    n_phys, page = cache_cmp_k.shape[:2]
    past = page_table.shape[1] * page
    win = state_win_k.shape[1]
    assert win == WINDOW and s % 512 == 0 and s >= WINDOW and past % SLC_BLOCK == 0
    width = a_w_out.shape[1]
    gd = width // A_GROUPS

    rows = b + bd
    rows_p = -(-rows // 8) * 8
    c_all = jnp.pad(jnp.concatenate([c_prompt, c_sample], axis=0), ((0, rows_p - rows), (0, 0)))
    mod = _ada(c_all, w_ada, b_ada)

    def mods(layer, lo, hi, per_token):
        m = mod[layer, lo:hi].reshape(hi - lo, 6, d)
        return [m[None, :, k, :] if per_token else m[:, None, k, :] for k in range(6)]

    w_in_a = a_w_in[0].astype(BF16)
    w_out_a = a_w_out[0].astype(BF16)
    vg = a_v_gain[0][None, :]
    bias_p = jnp.repeat(a_b_s[0].T, gd, axis=1)
    diag_s = jnp.repeat(a_w_s[0, :, 0, 0], gd)[None, :]
    bias_s = jnp.repeat(a_b_s[0, :, 0], gd)[None, :]
    w_gu = ffn_w_gu.astype(BF16)
    w_down = ffn_w_down.astype(BF16)
    w_kvt = w_kv.T.astype(BF16)
    w_qt = jnp.pad(b_w_in[0].T, ((0, GATE_PAD - 3 * N_HEADS), (0, 0))).astype(BF16)
    w_o = b_w_out[0].astype(BF16)
    kg1 = k_gains[1][:, None]
    kg2 = k_gains[2][:, None]
    qg = b_q_gain[0][:, None]
    cw_k = _compress_weights(0, cmp_pe, cmp_w1, cmp_b1, cmp_w2, cmp_b2, k_gains[0])
    cw_v = _compress_weights(1, cmp_pe, cmp_w1, cmp_b1, cmp_w2, cmp_b2, k_gains[0])
    ng = norm_g[:, :, None, :]

    tm = WINDOW
    sh1, sc1, gt1, sh2, sc2, gt2 = mods(0, 0, b, False)
    x1, v_p = _mixer_a(x_prompt, sh1, sc1, gt1, ng[0, 0], w_in_a, vg, a_w_s[0], bias_p, w_out_a,
                       single_token=False, tm=tm)
    x2 = _ffn(x1, sh2, sc2, gt2, ng[0, 1], w_gu[0], w_down[0], tm=tm)

    cos_p, sin_p = _rope_tables(jnp.arange(s, dtype=jnp.int32))
    kct_p, vct_p, kst_p, vst_p, kwt_p, vwt_p, ksb, vsb, kwb, vwb = _kv_proj(
        x2, kv_gain[None, :], w_kvt, kg1, kg2, cos_p, sin_p, tm=tm)
    kcc = _compress_prompt(kct_p, cw_k, norm=True)
    vcc = _compress_prompt(vct_p, cw_v, norm=False)

    sh1, sc1, gt1, sh2, sc2, gt2 = mods(1, 0, b, False)
    q_p, qr_p, gates_p = _q_proj(x2, sh1, sc1, ng[1, 0], w_qt, qg, cos_p, sin_p, tm=tm)
    ncp = s // CMP_STRIDE
    x3 = _attn_prompt(q_p, qr_p, gates_p, x2, gt1, kcc, vcc, ksb, vsb, kwb, vwb,
                      _expand_matrix(s) * NEG_INF, _overlap_matrix(ncp, ncp - 1),
                      w_o.reshape(N_HEADS, HEAD_DIM, d))
    y_prompt = _ffn(x3, sh2, sc2, gt2, ng[1, 1], w_gu[1], w_down[1], tm=tm)

    xs = x_sample.reshape(1, bd, d)
    sh1, sc1, gt1, sh2, sc2, gt2 = mods(0, b, b + bd, True)
    xs1, v_s = _mixer_a(xs, sh1, sc1, gt1, ng[0, 0], w_in_a, vg, diag_s, bias_s, w_out_a,
                        single_token=True, tm=bd)
    xs2 = _ffn(xs1, sh2, sc2, gt2, ng[0, 1], w_gu[0], w_down[0], tm=bd)

    cos_s, sin_s = _rope_tables(jnp.full((bd,), past, dtype=jnp.int32))
    new_t = _kv_proj(xs2, kv_gain[None, :], w_kvt, kg1, kg2, cos_s, sin_s, tm=bd)[:6]
    sh1, sc1, gt1, sh2, sc2, gt2 = mods(1, b, b + bd, True)
    q_s, qr_s, gates_s = _q_proj(xs2, sh1, sc1, ng[1, 0], w_qt, qg, cos_s, sin_s, tm=bd)

    def per_head(t):
        r = t[0].T.reshape(bd, KV_HEADS, 1, HEAD_DIM)
        return jnp.broadcast_to(r, (bd, KV_HEADS, HEADS_PER_GROUP, HEAD_DIM)).reshape(bd, N_HEADS, HEAD_DIM)

    def lane_pad(t):
        return jnp.pad(t.reshape(bd, N_HEADS, HEAD_DIM), ((0, 0), (0, 0), (0, LANES - HEAD_DIM)))

    caches = [c.transpose(0, 2, 3, 1).reshape(n_phys, KV_WIDTH, page)
              for c in (cache_cmp_k, cache_cmp_v, cache_slc_k, cache_slc_v)]
    win_k = state_win_k.transpose(0, 2, 3, 1).reshape(bd, KV_WIDTH, win)
    win_v = state_win_v.transpose(0, 2, 3, 1).reshape(bd, KV_WIDTH, win)
    new16 = [per_head(new_t[i]) for i in (2, 3, 4, 5)]
    new_cols = [new_t[i][0].T[:, :, None] for i in (4, 5)]
    gates16 = gates_s[0, :, :3 * N_HEADS].reshape(bd, N_HEADS, 3)
    o_s, wk_new, wv_new = _attn_sample(
        page_table, caches, win_k, win_v, lane_pad(q_s), lane_pad(qr_s), gates16, new16, new_cols,
        cw_k, cw_v, _expand_matrix(past), _overlap_matrix(past // CMP_STRIDE, past // CMP_STRIDE - 1))
    xs3 = _out_proj(o_s.reshape(1, bd, N_HEADS * HEAD_DIM), xs2, gt1, w_o)
    y_sample = _ffn(xs3, sh2, sc2, gt2, ng[1, 1], w_gu[1], w_down[1], tm=bd)

    new_rows =[t[0].T.reshape(bd, 1, KV_HEADS, HEAD_DIM) for t in new_t[:4]]
    return (y_prompt, y_sample.reshape(bd, 1, d), v_p[None], v_s.reshape(1, bd, 1, width),
            _to_rows(kct_p), _to_rows(vct_p), _to_rows(kst_p), _to_rows(vst_p),
            _to_rows(kwt_p), _to_rows(vwt_p),
            new_rows[0], new_rows[1], new_rows[2], new_rows[3],
            _to_rows(wk_new), _to_rows(wv_new))
```

```python
import functools

import numpy as np
import jax
import jax.numpy as jnp
from jax import lax
from jax.experimental import pallas as pl
from jax.experimental.pallas import tpu as pltpu

F32 = jnp.float32
BF16 = jnp.bfloat16

EPS = 1e-6
NEG_INF = -1e30
FORCE_SCORE = 1e9
TINY = 1e-30
ROPE_THETA = 10000.0
LOG2E = 1.4426950408889634
KNOCKED_OUT = -3e38
FORCED_MARK = -2e30
TAKEN_BELOW = -1e34

N_HEADS = 16
HEAD_DIM = 64
KV_HEADS = 4
HEADS_PER_GROUP = N_HEADS // KV_HEADS
KV_WIDTH = KV_HEADS * HEAD_DIM
V_AUG = HEAD_DIM + 16
CHUNK = 128
A_GROUPS = 8
CMP_STRIDE = 16
CMP_BLOCK = 2 * CMP_STRIDE
SLC_BLOCK = 64
N_SELECT = 16
WINDOW = 512
Q_BLOCK = 128
LANES = 128
GATE_PAD = 128
VMEM_LIMIT = 56 * 1024 * 1024

_NT = (((1,), (1,)), ((), ()))


def _params(n_grid):
    return pltpu.CompilerParams(dimension_semantics=("arbitrary",) * n_grid, vmem_limit_bytes=VMEM_LIMIT)


def _dot(a, b):
    return jnp.dot(a, b, preferred_element_type=F32)


def _dot_nt(a, b):
    return lax.dot_general(a, b, _NT, preferred_element_type=F32)


def _rms(x, g):
    return x * lax.rsqrt(jnp.mean(x * x, axis=-1, keepdims=True) + EPS) * g


def _split_dot(x, w):
    hi = x.astype(BF16)
    lo = (x - hi.astype(F32)).astype(BF16)
    return _dot(hi, w) + _dot(lo, w)


def _iota(shape, dim):
    return lax.broadcasted_iota(jnp.int32, shape, dim)


def _const_spec(shape):
    nd = len(shape)
    return pl.BlockSpec(shape, lambda *_: (0,) * nd)


def _mod_spec(m, tm):
    d = m.shape[-1]
    if m.shape[1] == 1:
        return pl.BlockSpec((1, 1, d), lambda b, j: (b, 0, 0))
    return pl.BlockSpec((1, tm, d), lambda b, j: (b, j, 0))


def _ada_kernel(c_ref, w_ref, b_ref, o_ref):
    c = c_ref[...]
    s = (c * jax.nn.sigmoid(c)).astype(BF16)
    o_ref[0] = _dot(s, w_ref[0].astype(BF16)) + b_ref[0]


def _ada(c_all, w_ada, b_ada, tn=1536):
    n_layers, d, d6 = w_ada.shape
    m = c_all.shape[0]
    return pl.pallas_call(
        _ada_kernel,
        grid=(n_layers, d6 // tn),
        in_specs=[pl.BlockSpec((m, d), lambda l, j: (0, 0)),
                  pl.BlockSpec((1, d, tn), lambda l, j: (l, 0, j)),
                  pl.BlockSpec((1, 1, tn), lambda l, j: (l, 0, j))],
        out_specs=pl.BlockSpec((1, m, tn), lambda l, j: (l, 0, j)),
        out_shape=jax.ShapeDtypeStruct((n_layers, m, d6), F32),
        compiler_params=_params(2),
        name="ada_modulation",
    )(c_all, w_ada, b_ada.reshape(n_layers, 1, d6))


def _mixer_a_kernel(x_ref, sh_ref, sc_ref, gt_ref, ng_ref, win_ref, vg_ref, ws_ref, bias_ref, wout_ref,
                    x1_ref, v_ref, um_ref, *, single_token):
    x = x_ref[0]
    h = _rms(x, ng_ref[...]) * (1.0 + sc_ref[0]) + sh_ref[0]
    act = jax.nn.gelu(_dot(h.astype(BF16), win_ref[...]))
    width = act.shape[1] // 2
    u = act[:, :width]
    v = _rms(act[:, width:], vg_ref[...])
    if single_token:
        v_ref[0] = v
        um_ref[...] = (u * (v * ws_ref[...] + bias_ref[...])).astype(BF16)
    else:
        tm = x.shape[0]
        gd = width // A_GROUPS

        @pl.when(pl.program_id(1) == pl.num_programs(1) - 1)
        def _():
            v_ref[0] = v[tm - CHUNK:, :]

        causal = _iota((CHUNK, CHUNK), 0) >= _iota((CHUNK, CHUNK), 1)
        for g in range(A_GROUPS):
            wg = jnp.where(causal, ws_ref[g], 0.0).astype(BF16)
            cs = slice(g * gd, (g + 1) * gd)
            chunks = [slice(c * CHUNK, (c + 1) * CHUNK) for c in range(tm // CHUNK)]
            mixed = _dot(wg, jnp.concatenate([v[rs, cs] for rs in chunks], axis=1).astype(BF16))
            for c, rs in enumerate(chunks):
                um_ref[rs, cs] = (u[rs, cs] * (mixed[:, c * gd:(c + 1) * gd] + bias_ref[:, cs])).astype(BF16)
    x1_ref[0] = x + gt_ref[0] * _dot(um_ref[...], wout_ref[...])


def _mixer_a(x, sh, sc, gt, ng, w_in, v_gain, ws, bias, w_out, *, single_token, tm):
    bm, sm, d = x.shape
    width = w_out.shape[0]
    v_rows = sm if single_token else CHUNK
    kern = functools.partial(_mixer_a_kernel, single_token=single_token)
    return pl.pallas_call(
        kern,
        grid=(bm, sm // tm),
        in_specs=[pl.BlockSpec((1, tm, d), lambda b, j: (b, j, 0)),
                  _mod_spec(sh, tm), _mod_spec(sc, tm), _mod_spec(gt, tm),
                  _const_spec(ng.shape), _const_spec(w_in.shape), _const_spec(v_gain.shape),
                  _const_spec(ws.shape), _const_spec(bias.shape), _const_spec(w_out.shape)],
        out_specs=[pl.BlockSpec((1, tm, d), lambda b, j: (b, j, 0)),
                   pl.BlockSpec((1, v_rows, width), lambda b, j: (b, 0, 0))],
        out_shape=[jax.ShapeDtypeStruct((bm, sm, d), F32),
                   jax.ShapeDtypeStruct((bm, v_rows, width), F32)],
        scratch_shapes=[pltpu.VMEM((tm, width), BF16)],
        compiler_params=_params(2),
        name="mixer_a",
    )(x, sh, sc, gt, ng, w_in, v_gain, ws, bias, w_out)


def _ffn_kernel(x_ref, sh_ref, sc_ref, gt_ref, ng_ref, wgu_ref, wd_ref, o_ref, act_ref, *, fc):
    x = x_ref[0]
    h = (_rms(x, ng_ref[...]) * (1.0 + sc_ref[0]) + sh_ref[0]).astype(BF16)
    dff = wd_ref.shape[0]
    for j in range(dff // fc):
        g = _dot(h, wgu_ref[:, j * fc:(j + 1) * fc])
        u = _dot(h, wgu_ref[:, dff + j * fc:dff + (j + 1) * fc])
        act_ref[:, j * fc:(j + 1) * fc] = (g * jax.nn.sigmoid(g) * u).astype(BF16)
    o_ref[0] = x + gt_ref[0] * _dot(act_ref[...], wd_ref[...])


def _ffn(x, sh, sc, gt, ng, w_gu, w_down, *, tm, fc=256):
    bm, sm, d = x.shape
    dff = w_down.shape[0]
    assert dff % fc == 0
    return pl.pallas_call(
        functools.partial(_ffn_kernel, fc=fc),
        grid=(bm, sm // tm),
        in_specs=[pl.BlockSpec((1, tm, d), lambda b, j: (b, j, 0)),
                  _mod_spec(sh, tm), _mod_spec(sc, tm), _mod_spec(gt, tm),
                  _const_spec(ng.shape), _const_spec(w_gu.shape), _const_spec(w_down.shape)],
        out_specs=pl.BlockSpec((1, tm, d), lambda b, j: (b, j, 0)),
        out_shape=jax.ShapeDtypeStruct((bm, sm, d), F32),
        scratch_shapes=[pltpu.VMEM((tm, dff), BF16)],
        compiler_params=_params(2),
        name="swiglu_ffn",
    )(x, sh, sc, gt, ng, w_gu, w_down)


def _head_norm_rope(t, gain_col, cos, sin):
    rows, n = t.shape
    t3 = t.reshape(rows // HEAD_DIM, HEAD_DIM, n)
    y = t3 * lax.rsqrt(jnp.mean(t3 * t3, axis=1, keepdims=True) + EPS) * gain_col[None]
    half = HEAD_DIM // 2
    y1 = y[:, :half, :]
    y2 = y[:, half:, :]
    c = cos[None]
    s = sin[None]
    rot = jnp.concatenate([y1 * c - y2 * s, y2 * c + y1 * s], axis=1)
    return y.reshape(rows, n), rot.reshape(rows, n)


def _kv_proj_kernel(x_ref, kvg_ref, w_ref, g1_ref, g2_ref, cos_ref, sin_ref,
                    kc_ref, vc_ref, ks_ref, vs_ref, kw_ref, vw_ref, ksb_ref, vsb_ref, kwb_ref, vwb_ref):
    xn = _rms(x_ref[0], kvg_ref[...]).astype(BF16)
    kvt = _dot_nt(w_ref[...], xn)
    cos = cos_ref[...]
    sin = sin_ref[...]

    def part(i):
        return kvt[i * KV_WIDTH:(i + 1) * KV_WIDTH]

    kc_ref[0] = part(0)
    vc_ref[0] = part(1)
    _, ks = _head_norm_rope(part(2), g1_ref[...], cos, sin)
    _, kw = _head_norm_rope(part(4), g2_ref[...], cos, sin)
    vs = part(3)
    vw = part(5)
    ks_ref[0] = ks
    vs_ref[0] = vs
    @pl.when(pl.program_id(1) == pl.num_programs(1) - 1)
    def _():
        kw_ref[0] = kw
        vw_ref[0] = vw
    ksb_ref[0] = ks.astype(BF16)
    kwb_ref[0] = kw.astype(BF16)
    ones = jnp.ones((V_AUG - HEAD_DIM, vs.shape[1]), BF16)
    for g in range(KV_HEADS):
        hs = slice(g * HEAD_DIM, (g + 1) * HEAD_DIM)
        vsb_ref[0, g, :HEAD_DIM] = vs[hs].astype(BF16)
        vsb_ref[0, g, HEAD_DIM:] = ones
        vwb_ref[0, g, :HEAD_DIM] = vw[hs].astype(BF16)
        vwb_ref[0, g, HEAD_DIM:] = ones


def _kv_proj(x, kv_gain, w_kvt, g1, g2, cos_t, sin_t, *, tm):
    bm, sm, d = x.shape
    half = HEAD_DIM // 2
    out_spec = pl.BlockSpec((1, KV_WIDTH, tm), lambda b, j: (b, 0, j))
    aug_spec = pl.BlockSpec((1, KV_HEADS, V_AUG, tm), lambda b, j: (b, 0, 0, j))
    last_spec = pl.BlockSpec((1, KV_WIDTH, tm), lambda b, j: (b, 0, 0))
    k_bf = jax.ShapeDtypeStruct((bm, KV_WIDTH, sm), BF16)
    v_bf = jax.ShapeDtypeStruct((bm, KV_HEADS, V_AUG, sm), BF16)
    return pl.pallas_call(
        _kv_proj_kernel,
        grid=(bm, sm // tm),
        in_specs=[pl.BlockSpec((1, tm, d), lambda b, j: (b, j, 0)),
                  _const_spec(kv_gain.shape), _const_spec(w_kvt.shape), _const_spec(g1.shape),
                  _const_spec(g2.shape),
                  pl.BlockSpec((half, tm), lambda b, j: (0, j)),
                  pl.BlockSpec((half, tm), lambda b, j: (0, j))],
        out_specs=[out_spec] * 4 + [last_spec] * 2 + [out_spec, aug_spec, out_spec, aug_spec],
        out_shape=[jax.ShapeDtypeStruct((bm, KV_WIDTH, sm), F32)] * 4
                  + [jax.ShapeDtypeStruct((bm, KV_WIDTH, tm), F32)] * 2 + [k_bf, v_bf, k_bf, v_bf],
        compiler_params=_params(2),
        name="kv_projection",
    )(x, kv_gain, w_kvt, g1, g2, cos_t, sin_t)


def _q_proj_kernel(x_ref, sh_ref, sc_ref, ng_ref, w_ref, qg_ref, cos_ref, sin_ref, q_ref, qr_ref, gates_ref):
    h = (_rms(x_ref[0], ng_ref[...]) * (1.0 + sc_ref[0]) + sh_ref[0]).astype(BF16)
    pt = _dot_nt(w_ref[...], h)
    nq = N_HEADS * HEAD_DIM
    qn, qrot = _head_norm_rope(pt[:nq], qg_ref[...], cos_ref[...], sin_ref[...])
    scale = LOG2E * HEAD_DIM ** -0.5
    q_ref[0] = (qn * scale).T.astype(BF16)
    qr_ref[0] = (qrot * scale).T.astype(BF16)
    gates_ref[0] = jax.nn.sigmoid(pt[nq:]).T


def _q_proj(x, sh, sc, ng, w_int, q_gain, cos_t, sin_t, *, tm):
    bm, sm, d = x.shape
    nq = N_HEADS * HEAD_DIM
    half = HEAD_DIM // 2
    return pl.pallas_call(
        _q_proj_kernel,
        grid=(bm, sm // tm),
        in_specs=[pl.BlockSpec((1, tm, d), lambda b, j: (b, j, 0)),
                  _mod_spec(sh, tm), _mod_spec(sc, tm),
                  _const_spec(ng.shape), _const_spec(w_int.shape), _const_spec(q_gain.shape),
                  pl.BlockSpec((half, tm), lambda b, j: (0, j)),
                  pl.BlockSpec((half, tm), lambda b, j: (0, j))],
        out_specs=[pl.BlockSpec((1, tm, nq), lambda b, j: (b, j, 0)),
                   pl.BlockSpec((1, tm, nq), lambda b, j: (b, j, 0)),
                   pl.BlockSpec((1, tm, GATE_PAD), lambda b, j: (b, j, 0))],
        out_shape=[jax.ShapeDtypeStruct((bm, sm, nq), BF16),
                   jax.ShapeDtypeStruct((bm, sm, nq), BF16),
                   jax.ShapeDtypeStruct((bm, sm, GATE_PAD), F32)],
        compiler_params=_params(2),
        name="q_projection",
    )(x, sh, sc, ng, w_int, q_gain, cos_t, sin_t)


def _store_rows(xs_ref, xt):
    for c in range(xs_ref.shape[0]):
        xs_ref[c] = xt[c * LANES:(c + 1) * LANES, :].T


def _compress_halves(xs_ref, n, pe_a, pe_b, w1a_ref, w1b_ref):
    low = _iota((n, LANES), 1) < HEAD_DIM
    per_head = [[] for _ in range(KV_HEADS)]
    for c in range(KV_WIDTH // LANES):
        for r in range(0, CMP_STRIDE, 2):
            x0 = xs_ref[c, pl.ds(r, n, stride=CMP_STRIDE), :]
            x1 = xs_ref[c, pl.ds(r + 1, n, stride=CMP_STRIDE), :]
            per_head[2 * c].append(jnp.where(low, x0, pltpu.roll(x1, shift=HEAD_DIM, axis=1)))
            per_head[2 * c + 1].append(jnp.where(low, pltpu.roll(x0, shift=HEAD_DIM, axis=1), x1))
    lhs = jnp.concatenate([jnp.concatenate(p, axis=1) for p in per_head], axis=0)
    a = _dot((lhs + pe_a).astype(BF16), w1a_ref[...])
    b = _dot((lhs + pe_b).astype(BF16), w1b_ref[...])
    return a, b


def _compress_finish(a, b_next, b1, w2_ref, b2, gain):
    hid = jax.nn.gelu(a + b_next + b1)
    out = _dot(hid.astype(BF16), w2_ref[...]) + b2
    if gain is not None:
        ms = jnp.sum(out * out, axis=-1, keepdims=True) * (1.0 / HEAD_DIM)
        out = out * lax.rsqrt(ms + EPS) * gain
    return out


def _compress_kernel(x_ref, nxt_ref, pea_ref, peb_ref, w1a_ref, w1b_ref, b1_ref, w2_ref, b2_ref, gain_ref,
                     o_ref, xs_ref, xn_ref, *, norm):
    lc = x_ref.shape[-1]
    n = lc // CMP_STRIDE
    nn = nxt_ref.shape[-1] // CMP_STRIDE
    _store_rows(xs_ref, x_ref[0])
    _store_rows(xn_ref, nxt_ref[0])
    a, b = _compress_halves(xs_ref, n, pea_ref[...], peb_ref[...], w1a_ref, w1b_ref)
    _, bn = _compress_halves(xn_ref, nn, pea_ref[...], peb_ref[...], w1a_ref, w1b_ref)
    last_row = _iota((n, 1), 0) == n - 1
    b_next = jnp.concatenate(
        [jnp.where(last_row, bn[g * nn:g * nn + 1, :], pltpu.roll(b[g * n:(g + 1) * n], shift=n - 1, axis=0))
         for g in range(KV_HEADS)], axis=0)
    out = _compress_finish(a, b_next, b1_ref[...], w2_ref, b2_ref[...], gain_ref[...] if norm else None)
    for g in range(KV_HEADS):
        o_ref[0, g] = out[g * n:(g + 1) * n].T[:HEAD_DIM]


def _compress_prompt(xt, cw, *, norm, lc=2048, ln=256):
    b, _, s = xt.shape
    lc = min(lc, s)
    n = lc // CMP_STRIDE
    n_next = s // ln
    return pl.pallas_call(
        functools.partial(_compress_kernel, norm=norm),
        grid=(b, s // lc),
        in_specs=[pl.BlockSpec((1, KV_WIDTH, lc), lambda i, c: (i, 0, c)),
                  pl.BlockSpec((1, KV_WIDTH, ln),
                               lambda i, c: (i, 0, jnp.minimum((c + 1) * (lc // ln), n_next - 1)))]
                 + [_const_spec(w.shape) for w in cw],
        out_specs=pl.BlockSpec((1, KV_HEADS, HEAD_DIM, n), lambda i, c: (i, 0, 0, c)),
        out_shape=jax.ShapeDtypeStruct((b, KV_HEADS, HEAD_DIM, s // CMP_STRIDE), F32),
        scratch_shapes=[pltpu.VMEM((KV_WIDTH // LANES, lc, LANES), F32),
                        pltpu.VMEM((KV_WIDTH // LANES, ln, LANES), F32)],
        compiler_params=_params(2),
        name="compress_prompt",
    )(xt, xt, *cw)


def _select_blocks(scores_t, n_sel):
    srow = _iota(scores_t[0].shape, 0).astype(F32)
    n_blocks = float(scores_t[0].shape[0])
    vals = list(scores_t)
    for _ in range(n_sel):
        for g, v in enumerate(vals):
            best = jnp.max(v, axis=0, keepdims=True)
            idx = jnp.min(jnp.where(v == best, srow, n_blocks), axis=0, keepdims=True)
            vals[g] = jnp.where(srow == idx, KNOCKED_OUT, v)
    return [jnp.where(v < TAKEN_BELOW, 1.0, 0.0) for v in vals]


def _attn_prompt_kernel(q_ref, qr_ref, gates_ref, x_ref, g1_ref, kct_ref, vct_ref, ks_ref, vs_ref, kw_ref,
                        vw_ref, nege_ref, ov_ref, wo_ref, o_ref, *, nc, n_sel, tk):
    n = pl.program_id(1)
    qb = Q_BLOCK
    hq = HEADS_PER_GROUP * qb
    q0 = n * qb
    ncp = kct_ref.shape[-1]
    qpos = q0 + _iota((qb, 1), 0)
    cur = qpos // SLC_BLOCK
    gates = gates_ref[0]
    q_all = q_ref[0]
    qr_all = qr_ref[0]

    ci = _iota((qb, ncp), 1)
    vis = (ci < nc) & (ci * CMP_STRIDE + (CMP_BLOCK - 1) <= qpos)
    vis_bias = jnp.where(vis, 0.0, NEG_INF)
    row_visible = jnp.where((qpos >= CMP_BLOCK - 1) & (nc > 0), 1.0, 0.0)
    blk = _iota((qb, LANES), 1)
    valid = blk <= cur
    forced = (blk == 0) | (blk == cur) | (blk == cur - 1)
    n_tiles = (q0 + qb + tk - 1) // tk

    def group_rows(x_all, g):
        return jnp.concatenate([x_all[:, h * HEAD_DIM:(h + 1) * HEAD_DIM]
                                for h in range(g * HEADS_PER_GROUP, (g + 1) * HEADS_PER_GROUP)], axis=0)

    ocs, imps = [], []
    for g in range(KV_HEADS):
        s = _dot(group_rows(q_all, g), kct_ref[0, g].astype(BF16)).reshape(HEADS_PER_GROUP, qb, ncp)
        s = s + vis_bias[None]
        p = jnp.exp2(s - jnp.max(s, axis=-1, keepdims=True))
        pc = p * (row_visible[None] / jnp.maximum(jnp.sum(p, axis=-1, keepdims=True), TINY))
        ocs.append(_dot_nt(pc.reshape(hq, ncp).astype(BF16), vct_ref[0, g].astype(BF16)))
        imps.append(_split_dot(pc[0] + pc[1] + pc[2] + pc[3], ov_ref[...]))

    k_last = (n_tiles - 1) * tk
    causal_bias = jnp.where(k_last + _iota((qb, tk), 1) <= qpos, 0.0, NEG_INF)
    w_tiles = WINDOW // qb + 1
    w_offs, w_biases = [], []
    for t in range(w_tiles):
        off = q0 - WINDOW + qb * t
        w_offs.append(pl.multiple_of(jnp.maximum(off, 0), qb))
        wpos = off + _iota((qb, qb), 1)
        dpos = qpos - wpos
        w_biases.append(jnp.where((dpos >= 0) & (dpos <= WINDOW) & (wpos >= 0), 0.0, NEG_INF))
    w_bias = jnp.concatenate(w_biases, axis=1)
    wk = w_tiles * qb

    qrgs = [group_rows(qr_all, g) for g in range(KV_HEADS)]
    acc_ws = []
    for g in range(KV_HEADS):
        hs = slice(g * HEAD_DIM, (g + 1) * HEAD_DIM)
        kwt = jnp.concatenate([kw_ref[0, hs, pl.ds(o, qb)] for o in w_offs], axis=1)
        vwt = jnp.concatenate([vw_ref[0, g, :, pl.ds(o, qb)] for o in w_offs], axis=1)
        sw = (_dot(qrgs[g], kwt).reshape(HEADS_PER_GROUP, qb, wk) + w_bias[None]).reshape(hq, wk)
        pw = jnp.exp2(sw - jnp.max(sw, axis=-1, keepdims=True)).astype(BF16)
        acc_ws.append(_dot_nt(pw, vwt))

    scores = [jnp.where(valid, jnp.where(forced, FORCED_MARK, imp), NEG_INF).T for imp in imps]
    sel_t = _select_blocks(scores, n_sel - 3)
    forced_one = jnp.where(forced, 1.0, 0.0)

    lhss = []
    for g in range(KV_HEADS):
        not_sel = jnp.where(valid, 1.0 - jnp.maximum(sel_t[g].T, forced_one), 1.0).astype(BF16)
        lhss.append(jnp.concatenate([jnp.concatenate([not_sel] * HEADS_PER_GROUP, axis=0), qrgs[g]], axis=1))

    def tile(k0, width, carry, causal):
        neg = nege_ref[:, pl.ds(k0, width)]
        out = []
        for g in range(KV_HEADS):
            m_i, acc = carry[g]
            rhs = jnp.concatenate([neg, ks_ref[0, g * HEAD_DIM:(g + 1) * HEAD_DIM, pl.ds(k0, width)]], axis=0)
            st = _dot(lhss[g], rhs)
            if causal:
                st = (st.reshape(HEADS_PER_GROUP, qb, width) + causal_bias[None]).reshape(hq, width)
            m_new = jnp.maximum(m_i, jnp.max(st, axis=-1, keepdims=True))
            pt = jnp.exp2(st - m_new).astype(BF16)
            acc = jnp.exp2(m_i - m_new) * acc + _dot_nt(pt, vs_ref[0, g, :, pl.ds(k0, width)])
            out.append((m_new, acc))
        return tuple(out)

    n_full = n_tiles - 1
    one = n_full % 2
    two = (n_full // 2) % 2
    init = tuple((jnp.full((hq, 1), NEG_INF, F32), jnp.zeros((hq, V_AUG), F32)) for _ in range(KV_HEADS))
    carry = lax.cond(one == 1, lambda c: tile(0, tk, c, False), lambda c: c, init)
    carry = lax.cond(two == 1, lambda c: tile(pl.multiple_of(one * tk, tk), 2 * tk, c, False), lambda c: c, carry)
    carry = lax.fori_loop(
        0, n_full // 4,
        lambda j, c: tile(pl.multiple_of((one + 2 * two + 4 * j) * tk, tk), 4 * tk, c, False), carry)
    accs = [c[1] for c in tile(pl.multiple_of(n_full * tk, tk), tk, carry, True)]

    proj = jnp.zeros((qb, wo_ref.shape[-1]), F32)
    for g in range(KV_HEADS):
        acc_s = accs[g]
        acc_w = acc_ws[g]
        for i in range(HEADS_PER_GROUP):
            h = g * HEADS_PER_GROUP + i
            rs = slice(i * qb, (i + 1) * qb)
            gs = gates[:, 3 * h + 1:3 * h + 2] / jnp.maximum(acc_s[rs, HEAD_DIM:HEAD_DIM + 1], TINY)
            gw = gates[:, 3 * h + 2:3 * h + 3] / jnp.maximum(acc_w[rs, HEAD_DIM:HEAD_DIM + 1], TINY)
            o_h = (ocs[g][rs] * gates[:, 3 * h:3 * h + 1] + acc_s[rs, :HEAD_DIM] * gs
                   + acc_w[rs, :HEAD_DIM] * gw)
            proj = proj + _dot(o_h.astype(BF16), wo_ref[h])
    o_ref[0] = x_ref[0] + g1_ref[0] * proj


def _attn_prompt(q, qr, gates, x, gate1, kct, vct, ksb, vsb, kwb, vwb, nege, ov, wo, *, tk=512):
    b, s, d = x.shape
    nq = q.shape[-1]
    ncp = kct.shape[-1]
    nc = s // CMP_STRIDE - 1
    ns = -(-s // SLC_BLOCK)
    tk = min(tk, s)
    kern = functools.partial(_attn_prompt_kernel, nc=nc, n_sel=min(N_SELECT, ns), tk=tk)
    full_k = pl.BlockSpec((1, KV_WIDTH, s), lambda i, j: (i, 0, 0), pipeline_mode=pl.Buffered(1))
    full_v = pl.BlockSpec((1, KV_HEADS, V_AUG, s), lambda i, j: (i, 0, 0, 0), pipeline_mode=pl.Buffered(1))
    cmp_spec = pl.BlockSpec((1, KV_HEADS, HEAD_DIM, ncp), lambda i, j: (i, 0, 0, 0))
    return pl.pallas_call(
        kern,
        grid=(b, s // Q_BLOCK),
        in_specs=[pl.BlockSpec((1, Q_BLOCK, nq), lambda i, j: (i, j, 0)),
                  pl.BlockSpec((1, Q_BLOCK, nq), lambda i, j: (i, j, 0)),
                  pl.BlockSpec((1, Q_BLOCK, GATE_PAD), lambda i, j: (i, j, 0)),
                  pl.BlockSpec((1, Q_BLOCK, d), lambda i, j: (i, j, 0)),
                  pl.BlockSpec((1, 1, d), lambda i, j: (i, 0, 0)),
                  cmp_spec, cmp_spec, full_k, full_v, full_k, full_v,
                  pl.BlockSpec(nege.shape, lambda i, j: (0, 0), pipeline_mode=pl.Buffered(1)),
                  _const_spec(ov.shape), _const_spec(wo.shape)],
        out_specs=pl.BlockSpec((1, Q_BLOCK, d), lambda i, j: (i, j, 0)),
        out_shape=jax.ShapeDtypeStruct((b, s, d), F32),
        compiler_params=_params(2),
        name="nsa_prompt",
    )(q, qr, gates, x, gate1, kct, vct, ksb, vsb, kwb, vwb, nege, ov, wo)


def _attn_sample_kernel(pt_ref, ck_ref, cv_ref, sk_ref, sv_ref, wk_ref, wv_ref, q_ref, qr_ref, gates_ref,
                        ksn_ref, vsn_ref, kwn_ref, vwn_ref, kwc_ref, vwc_ref,
                        kpea_ref, kpeb_ref, kw1a_ref, kw1b_ref, kb1_ref, kw2_ref, kb2_ref, kgain_ref,
                        vpea_ref, vpeb_ref, vw1a_ref, vw1b_ref, vb1_ref, vw2_ref, vb2_ref, vgain_ref,
                        e_ref, ov_ref,
                        o_ref, wko_ref, wvo_ref,
                        slabs, xs_ref, sems, *, past, page, n_sel):
    del vgain_ref
    b = pl.program_id(0)
    slot = b % 2
    caches = (ck_ref, cv_ref, sk_ref, sv_ref)

    def page_copy(seq, slot_, t, p):
        return pltpu.make_async_copy(caches[t].at[pt_ref[seq, p]],
                                     slabs.at[slot_, t, :, pl.ds(p * page, page)], sems.at[slot_, t])

    def gather(seq, slot_, start):
        for t in range(len(caches)):
            for p in range(past // page):
                cp = page_copy(seq, slot_, t, p)
                cp.start() if start else cp.wait()

    @pl.when(b == 0)
    def _():
        gather(0, 0, True)

    @pl.when(b + 1 < pl.num_programs(0))
    def _():
        gather(b + 1, 1 - slot, True)

    gather(b, slot, False)
    sl_ck, sl_cv, sl_sk, sl_sv = (slabs.at[slot, t] for t in range(len(caches)))

    def compute():
        nh = N_HEADS
        qpos = past
        n = past // CMP_STRIDE
        nc = n - 1
        cur = qpos // SLC_BLOCK
        win = wk_ref.shape[-1]
        hrow = _iota((nh, 1), 0) // HEADS_PER_GROUP

        def by_group(fn, width):
            out = jnp.zeros((nh, width), F32)
            for g in range(KV_HEADS):
                out = jnp.where(hrow == g, fn(g), out)
            return out

        def compress(slab, pea, peb, w1a, w1b, b1, w2, b2, gain):
            _store_rows(xs_ref, slab[...])
            a, bh = _compress_halves(xs_ref, n, pea[...], peb[...], w1a, w1b)
            b_next = jnp.concatenate([pltpu.roll(bh[g * n:(g + 1) * n], shift=n - 1, axis=0)
                                      for g in range(KV_HEADS)], axis=0)
            out = _compress_finish(a, b_next, b1[...], w2, b2[...], gain)
            return [out[g * n:(g + 1) * n] for g in range(KV_HEADS)]

        kc = compress(sl_ck, kpea_ref, kpeb_ref, kw1a_ref, kw1b_ref, kb1_ref, kw2_ref, kb2_ref, kgain_ref[...])
        vc = compress(sl_cv, vpea_ref, vpeb_ref, vw1a_ref, vw1b_ref, vb1_ref, vw2_ref, vb2_ref, None)

        q = q_ref[0]
        qr = qr_ref[0]
        qr_lo = qr[:, :HEAD_DIM]
        qr_f = qr_lo.astype(F32)

        sc = by_group(lambda g: _dot_nt(q, kc[g].astype(BF16)), n)
        ci = _iota((nh, n), 1)
        vis = (ci < nc) & (ci * CMP_STRIDE + (CMP_BLOCK - 1) <= qpos)
        sc = jnp.where(vis, sc, NEG_INF)
        pc = jnp.where(vis, jnp.exp2(sc - jnp.max(sc, axis=-1, keepdims=True)), 0.0)
        pc = pc / jnp.maximum(jnp.sum(pc, axis=-1, keepdims=True), TINY)
        o_c = by_group(lambda g: _dot(pc.astype(BF16), vc[g].astype(BF16)), LANES)[:, :HEAD_DIM]

        grow = _iota((nh, 1), 0)
        psum = jnp.zeros((nh, n), F32)
        for g in range(KV_HEADS):
            pg = jnp.sum(pc[g * HEADS_PER_GROUP:(g + 1) * HEADS_PER_GROUP], axis=0, keepdims=True)
            psum = jnp.where(grow == g, pg, psum)
        imp = _split_dot(psum, ov_ref[...])
        blk = _iota((nh, LANES), 1)
        valid = blk <= cur
        forced = (blk == 0) | (blk == cur) | (blk == cur - 1)
        score = jnp.where(valid, jnp.where(forced, FORCE_SCORE, imp), NEG_INF)
        rank = jnp.zeros((nh, LANES), F32)
        for sp in range(cur + 1):
            col = score[:, sp:sp + 1]
            rank = rank + jnp.where(col > score, 1.0, jnp.where(col == score, jnp.where(sp < blk, 1.0, 0.0), 0.0))
        sel_g = jnp.where(valid, jnp.where(rank < n_sel, 1.0, 0.0), 0.0)
        sel = by_group(lambda g: sel_g[g:g + 1, :], LANES)

        chosen = _dot(sel.astype(BF16), e_ref[...])
        kpos = _iota((nh, past), 1)
        bias = jnp.where(kpos <= qpos, jnp.where(chosen > 0.5, 0.0, NEG_INF), NEG_INF)
        ss = by_group(lambda g: _dot(qr_lo, sl_sk[g * HEAD_DIM:(g + 1) * HEAD_DIM, :].astype(BF16)), past) + bias
        s_new = jnp.sum(qr_f * ksn_ref[0], axis=-1, keepdims=True)
        s_new = jnp.where(sel[:, cur:cur + 1] > 0.5, s_new, NEG_INF)
        m = jnp.maximum(jnp.max(ss, axis=-1, keepdims=True), s_new)
        ps = jnp.exp2(ss - m)
        p_new = jnp.exp2(s_new - m)
        den = jnp.maximum(jnp.sum(ps, axis=-1, keepdims=True) + p_new, TINY)
        psb = ps.astype(BF16)
        o_s = by_group(lambda g: _dot_nt(psb, sl_sv[g * HEAD_DIM:(g + 1) * HEAD_DIM, :].astype(BF16)), HEAD_DIM)
        o_s = (o_s + p_new * vsn_ref[0]) / den

        wpos = qpos - win + _iota((nh, win), 1)
        dpos = qpos - wpos
        okw = (dpos >= 0) & (dpos <= WINDOW) & (wpos >= 0)
        sw = by_group(lambda g: _dot(qr_lo, wk_ref[0, g * HEAD_DIM:(g + 1) * HEAD_DIM, :].astype(BF16)), win)
        sw = sw + jnp.where(okw, 0.0, NEG_INF)
        sw_new = jnp.sum(qr_f * kwn_ref[0], axis=-1, keepdims=True)
        mw = jnp.maximum(jnp.max(sw, axis=-1, keepdims=True), sw_new)
        pw = jnp.exp2(sw - mw)
        pw_new = jnp.exp2(sw_new - mw)
        denw = jnp.maximum(jnp.sum(pw, axis=-1, keepdims=True) + pw_new, TINY)
        pwb = pw.astype(BF16)
        o_w = by_group(lambda g: _dot_nt(pwb, wv_ref[0, g * HEAD_DIM:(g + 1) * HEAD_DIM, :].astype(BF16)), HEAD_DIM)
        o_w = (o_w + pw_new * vwn_ref[0]) / denw

        gates = gates_ref[0]
        o_ref[0] = o_c * gates[:, 0:1] + o_s * gates[:, 1:2] + o_w * gates[:, 2:3]

        lane = _iota((KV_WIDTH, win), 1)
        wko_ref[0] = jnp.where(lane == win - 1, kwc_ref[0], pltpu.roll(wk_ref[0], shift=win - 1, axis=1))
        wvo_ref[0] = jnp.where(lane == win - 1, vwc_ref[0], pltpu.roll(wv_ref[0], shift=win - 1, axis=1))

    compute()


def _attn_sample(page_table, caches, win_k, win_v, q, qr, gates, new16, new_cols, cw_k, cw_v, e_mat, ov):
    bd, n_pages = page_table.shape
    page = caches[0].shape[-1]
    past = n_pages * page
    win = win_k.shape[-1]
    cur = past // SLC_BLOCK
    kern = functools.partial(_attn_sample_kernel, past=past, page=page, n_sel=min(N_SELECT, cur + 1))

    def per_seq(a):
        nd = a.ndim
        return pl.BlockSpec((1,) + a.shape[1:], lambda b, pt: (b,) + (0,) * (nd - 1))

    def const(a):
        nd = a.ndim
        return pl.BlockSpec(a.shape, lambda b, pt: (0,) * nd)

    consts = list(cw_k) + list(cw_v) + [e_mat, ov]
    grid_spec = pltpu.PrefetchScalarGridSpec(
        num_scalar_prefetch=1,
        grid=(bd,),
        in_specs=[pl.BlockSpec(memory_space=pl.ANY)] * len(caches)
                 + [per_seq(win_k), per_seq(win_v), per_seq(q), per_seq(qr), per_seq(gates)]
                 + [per_seq(a) for a in new16] + [per_seq(a) for a in new_cols] + [const(a) for a in consts],
        out_specs=[pl.BlockSpec((1, N_HEADS, HEAD_DIM), lambda b, pt: (b, 0, 0)),
                   pl.BlockSpec((1, KV_WIDTH, win), lambda b, pt: (b, 0, 0)),
                   pl.BlockSpec((1, KV_WIDTH, win), lambda b, pt: (b, 0, 0))],
        scratch_shapes=[pltpu.VMEM((2, len(caches), KV_WIDTH, past), F32),
                        pltpu.VMEM((KV_WIDTH // LANES, past, LANES), F32),
                        pltpu.SemaphoreType.DMA((2, len(caches)))],
    )
    return pl.pallas_call(
        kern,
        grid_spec=grid_spec,
        out_shape=[jax.ShapeDtypeStruct((bd, N_HEADS, HEAD_DIM), F32),
                   jax.ShapeDtypeStruct((bd, KV_WIDTH, win), F32),
                   jax.ShapeDtypeStruct((bd, KV_WIDTH, win), F32)],
        compiler_params=_params(1),
        name="nsa_decode",
    )(page_table, *caches, win_k, win_v, q, qr, gates, *new16, *new_cols, *consts)


def _out_proj_kernel(o_ref, x_ref, g_ref, w_ref, y_ref):
    y_ref[0] = x_ref[0] + g_ref[0] * _dot(o_ref[0].astype(BF16), w_ref[...])


def _out_proj(o, x, gate, w):
    bm, sm, d = x.shape
    spec = pl.BlockSpec((1, sm, d), lambda b: (b, 0, 0))
    return pl.pallas_call(
        _out_proj_kernel,
        grid=(bm,),
        in_specs=[pl.BlockSpec((1, sm, o.shape[-1]), lambda b: (b, 0, 0)), spec, spec, _const_spec(w.shape)],
        out_specs=spec,
        out_shape=jax.ShapeDtypeStruct((bm, sm, d), F32),
        compiler_params=_params(1),
        name="attn_out_proj",
    )(o, x, gate, w)


def _rope_tables(pos):
    half = HEAD_DIM // 2
    inv_freq = ROPE_THETA ** (-jnp.arange(half, dtype=F32) / half)
    ang = inv_freq[:, None] * pos.astype(F32)[None, :]
    return jnp.cos(ang), jnp.sin(ang)


def _overlap_matrix(n_rows, nc):
    i = np.arange(n_rows)[:, None] * CMP_STRIDE
    s = np.arange(LANES)[None, :] * SLC_BLOCK
    ov = (i < s + SLC_BLOCK) & (i + CMP_BLOCK > s) & (np.arange(n_rows)[:, None] < nc)
    return jnp.asarray(ov.astype(np.float32), dtype=BF16)


def _expand_matrix(n_keys):
    e = (np.arange(n_keys)[None, :] // SLC_BLOCK) == np.arange(LANES)[:, None]
    return jnp.asarray(e.astype(np.float32), dtype=BF16)


def _compress_weights(t, cmp_pe, cmp_w1, cmp_b1, cmp_w2, cmp_b2, gain):
    half = CMP_STRIDE * HEAD_DIM
    pad = LANES - HEAD_DIM
    return (cmp_pe[t, :CMP_STRIDE].reshape(1, half), cmp_pe[t, CMP_STRIDE:].reshape(1, half),
            cmp_w1[t, :half].astype(BF16), cmp_w1[t, half:].astype(BF16), cmp_b1[t][None, :],
            jnp.pad(cmp_w2[t], ((0, 0), (0, pad))).astype(BF16), jnp.pad(cmp_b2[t], (0, pad))[None, :],
            jnp.pad(gain, (0, pad))[None, :])


def _to_rows(t):
    b, _, n = t.shape
    return t.reshape(b, KV_HEADS, HEAD_DIM, n).transpose(0, 3, 1, 2)


def kernel(x_prompt, x_sample, cache_cmp_k, cache_cmp_v, cache_slc_k, cache_slc_v, state_win_k, state_win_v,
           page_table, c_prompt, c_sample, w_ada, b_ada, norm_g, a_w_in, a_v_gain, a_w_s, a_b_s, a_w_out,
           b_w_in, b_q_gain, b_w_out, kv_gain, w_kv, k_gains, cmp_pe, cmp_w1, cmp_b1, cmp_w2, cmp_b2,
           ffn_w_gu, ffn_w_down):
    b, s, d = x_prompt.shape
    bd = x_sample.shape[0]
    assert x_sample.shape[1] == 1 and w_ada.shape[0] == 2 and a_w_in.shape[0] == 1 and b_w_in.shape[0] == 1
    n_phys, page = cache_cmp_k.shape[:2]
    past = page_table.shape[1] * page
    win = state_win_k.shape[1]
    assert win == WINDOW and s % 512 == 0 and s >= WINDOW and past % SLC_BLOCK == 0
    width = a_w_out.shape[1]
    gd = width // A_GROUPS

    rows = b + bd
    rows_p = -(-rows // 8) * 8
    c_all = jnp.pad(jnp.concatenate([c_prompt, c_sample], axis=0), ((0, rows_p - rows), (0, 0)))
    mod = _ada(c_all, w_ada, b_ada)

    def mods(layer, lo, hi, per_token):
        m = mod[layer, lo:hi].reshape(hi - lo, 6, d)
        return [m[None, :, k, :] if per_token else m[:, None, k, :] for k in range(6)]

    w_in_a = a_w_in[0].astype(BF16)
    w_out_a = a_w_out[0].astype(BF16)
    vg = a_v_gain[0][None, :]
    bias_p = jnp.repeat(a_b_s[0].T, gd, axis=1)
    diag_s = jnp.repeat(a_w_s[0, :, 0, 0], gd)[None, :]
    bias_s = jnp.repeat(a_b_s[0, :, 0], gd)[None, :]
    w_gu = ffn_w_gu.astype(BF16)
    w_down = ffn_w_down.astype(BF16)
    w_kvt = w_kv.T.astype(BF16)
    w_qt = jnp.pad(b_w_in[0].T, ((0, GATE_PAD - 3 * N_HEADS), (0, 0))).astype(BF16)
    w_o = b_w_out[0].astype(BF16)
    kg1 = k_gains[1][:, None]
    kg2 = k_gains[2][:, None]
    qg = b_q_gain[0][:, None]
    cw_k = _compress_weights(0, cmp_pe, cmp_w1, cmp_b1, cmp_w2, cmp_b2, k_gains[0])
    cw_v = _compress_weights(1, cmp_pe, cmp_w1, cmp_b1, cmp_w2, cmp_b2, k_gains[0])
    ng = norm_g[:, :, None, :]

    tm = WINDOW
    sh1, sc1, gt1, sh2, sc2, gt2 = mods(0, 0, b, False)
    x1, v_p = _mixer_a(x_prompt, sh1, sc1, gt1, ng[0, 0], w_in_a, vg, a_w_s[0], bias_p, w_out_a,
                       single_token=False, tm=tm)
    x2 = _ffn(x1, sh2, sc2, gt2, ng[0, 1], w_gu[0], w_down[0], tm=tm)

    cos_p, sin_p = _rope_tables(jnp.arange(s, dtype=jnp.int32))
    kct_p, vct_p, kst_p, vst_p, kwt_p, vwt_p, ksb, vsb, kwb, vwb = _kv_proj(
        x2, kv_gain[None, :], w_kvt, kg1, kg2, cos_p, sin_p, tm=tm)
    kcc = _compress_prompt(kct_p, cw_k, norm=True)
    vcc = _compress_prompt(vct_p, cw_v, norm=False)

    sh1, sc1, gt1, sh2, sc2, gt2 = mods(1, 0, b, False)
    q_p, qr_p, gates_p = _q_proj(x2, sh1, sc1, ng[1, 0], w_qt, qg, cos_p, sin_p, tm=tm)
    ncp = s // CMP_STRIDE
    x3 = _attn_prompt(q_p, qr_p, gates_p, x2, gt1, kcc, vcc, ksb, vsb, kwb, vwb,
                      _expand_matrix(s) * NEG_INF, _overlap_matrix(ncp, ncp - 1),
                      w_o.reshape(N_HEADS, HEAD_DIM, d))
    y_prompt = _ffn(x3, sh2, sc2, gt2, ng[1, 1], w_gu[1], w_down[1], tm=tm)

    xs = x_sample.reshape(1, bd, d)
    sh1, sc1, gt1, sh2, sc2, gt2 = mods(0, b, b + bd, True)
    xs1, v_s = _mixer_a(xs, sh1, sc1, gt1, ng[0, 0], w_in_a, vg, diag_s, bias_s, w_out_a,
                        single_token=True, tm=bd)
    xs2 = _ffn(xs1, sh2, sc2, gt2, ng[0, 1], w_gu[0], w_down[0], tm=bd)

    cos_s, sin_s = _rope_tables(jnp.full((bd,), past, dtype=jnp.int32))
    new_t = _kv_proj(xs2, kv_gain[None, :], w_kvt, kg1, kg2, cos_s, sin_s, tm=bd)[:6]
    sh1, sc1, gt1, sh2, sc2, gt2 = mods(1, b, b + bd, True)
    q_s, qr_s, gates_s = _q_proj(xs2, sh1, sc1, ng[1, 0], w_qt, qg, cos_s, sin_s, tm=bd)

    def per_head(t):
        r = t[0].T.reshape(bd, KV_HEADS, 1, HEAD_DIM)
        return jnp.broadcast_to(r, (bd, KV_HEADS, HEADS_PER_GROUP, HEAD_DIM)).reshape(bd, N_HEADS, HEAD_DIM)

    def lane_pad(t):
        return jnp.pad(t.reshape(bd, N_HEADS, HEAD_DIM), ((0, 0), (0, 0), (0, LANES - HEAD_DIM)))

    caches = [c.transpose(0, 2, 3, 1).reshape(n_phys, KV_WIDTH, page)
              for c in (cache_cmp_k, cache_cmp_v, cache_slc_k, cache_slc_v)]
    win_k = state_win_k.transpose(0, 2, 3, 1).reshape(bd, KV_WIDTH, win)
    win_v = state_win_v.transpose(0, 2, 3, 1).reshape(bd, KV_WIDTH, win)
    new16 = [per_head(new_t[i]) for i in (2, 3, 4, 5)]
    new_cols = [new_t[i][0].T[:, :, None] for i in (4, 5)]
    gates16 = gates_s[0, :, :3 * N_HEADS].reshape(bd, N_HEADS, 3)
    o_s, wk_new, wv_new = _attn_sample(
        page_table, caches, win_k, win_v, lane_pad(q_s), lane_pad(qr_s), gates16, new16, new_cols,
        cw_k, cw_v, _expand_matrix(past), _overlap_matrix(past // CMP_STRIDE, past // CMP_STRIDE - 1))
    xs3 = _out_proj(o_s.reshape(1, bd, N_HEADS * HEAD_DIM), xs2, gt1, w_o)
    y_sample = _ffn(xs3, sh2, sc2, gt2, ng[1, 1], w_gu[1], w_down[1], tm=bd)

    new_rows =[t[0].T.reshape(bd, 1, KV_HEADS, HEAD_DIM) for t in new_t[:4]]
    return (y_prompt, y_sample.reshape(bd, 1, d), v_p[None], v_s.reshape(1, bd, 1, width),
            _to_rows(kct_p), _to_rows(vct_p), _to_rows(kst_p), _to_rows(vst_p),
            _to_rows(kwt_p), _to_rows(vwt_p),
            new_rows[0], new_rows[1], new_rows[2], new_rows[3],
            _to_rows(wk_new), _to_rows(wv_new))
```

```python
import functools

import numpy as np
import jax
import jax.numpy as jnp
from jax import lax
from jax.experimental import pallas as pl
from jax.experimental.pallas import tpu as pltpu

F32 = jnp.float32
BF16 = jnp.bfloat16

EPS = 1e-6
NEG_INF = -1e30
FORCE_SCORE = 1e9
TINY = 1e-30
ROPE_THETA = 10000.0
LOG2E = 1.4426950408889634
KNOCKED_OUT = -3e38
FORCED_MARK = -2e30
TAKEN_BELOW = -1e34

N_HEADS = 16
HEAD_DIM = 64
KV_HEADS = 4
HEADS_PER_GROUP = N_HEADS // KV_HEADS
KV_WIDTH = KV_HEADS * HEAD_DIM
V_AUG = HEAD_DIM + 16
CHUNK = 128
A_GROUPS = 8
CMP_STRIDE = 16
CMP_BLOCK = 2 * CMP_STRIDE
SLC_BLOCK = 64
N_SELECT = 16
WINDOW = 512
Q_BLOCK = 128
LANES = 128
GATE_PAD = 128
VMEM_LIMIT = 56 * 1024 * 1024

_NT = (((1,), (1,)), ((), ()))


def _params(n_grid):
    return pltpu.CompilerParams(dimension_semantics=("arbitrary",) * n_grid, vmem_limit_bytes=VMEM_LIMIT)


def _dot(a, b):
    return jnp.dot(a, b, preferred_element_type=F32)


def _dot_nt(a, b):
    return lax.dot_general(a, b, _NT, preferred_element_type=F32)


def _rms(x, g):
    return x * lax.rsqrt(jnp.mean(x * x, axis=-1, keepdims=True) + EPS) * g


def _split_dot(x, w):
    hi = x.astype(BF16)
    lo = (x - hi.astype(F32)).astype(BF16)
    return _dot(hi, w) + _dot(lo, w)


def _iota(shape, dim):
    return lax.broadcasted_iota(jnp.int32, shape, dim)


def _const_spec(shape):
    nd = len(shape)
    return pl.BlockSpec(shape, lambda *_: (0,) * nd)


def _mod_spec(m, tm):
    d = m.shape[-1]
    if m.shape[1] == 1:
        return pl.BlockSpec((1, 1, d), lambda b, j: (b, 0, 0))
    return pl.BlockSpec((1, tm, d), lambda b, j: (b, j, 0))


def _ada_kernel(c_ref, w_ref, b_ref, o_ref):
    c = c_ref[...]
    s = (c * jax.nn.sigmoid(c)).astype(BF16)
    o_ref[0] = _dot(s, w_ref[0].astype(BF16)) + b_ref[0]


def _ada(c_all, w_ada, b_ada, tn=1536):
    n_layers, d, d6 = w_ada.shape
    m = c_all.shape[0]
    return pl.pallas_call(
        _ada_kernel,
        grid=(n_layers, d6 // tn),
        in_specs=[pl.BlockSpec((m, d), lambda l, j: (0, 0)),
                  pl.BlockSpec((1, d, tn), lambda l, j: (l, 0, j)),
                  pl.BlockSpec((1, 1, tn), lambda l, j: (l, 0, j))],
        out_specs=pl.BlockSpec((1, m, tn), lambda l, j: (l, 0, j)),
        out_shape=jax.ShapeDtypeStruct((n_layers, m, d6), F32),
        compiler_params=_params(2),
        name="ada_modulation",
    )(c_all, w_ada, b_ada.reshape(n_layers, 1, d6))


def _mixer_a_kernel(x_ref, sh_ref, sc_ref, gt_ref, ng_ref, win_ref, vg_ref, ws_ref, bias_ref, wout_ref,
                    x1_ref, v_ref, um_ref, *, single_token):
    x = x_ref[0]
    h = _rms(x, ng_ref[...]) * (1.0 + sc_ref[0]) + sh_ref[0]
    act = jax.nn.gelu(_dot(h.astype(BF16), win_ref[...]))
    width = act.shape[1] // 2
    u = act[:, :width]
    v = _rms(act[:, width:], vg_ref[...])
    if single_token:
        v_ref[0] = v
        um_ref[...] = (u * (v * ws_ref[...] + bias_ref[...])).astype(BF16)
    else:
        tm = x.shape[0]
        gd = width // A_GROUPS

        @pl.when(pl.program_id(1) == pl.num_programs(1) - 1)
        def _():
            v_ref[0] = v[tm - CHUNK:, :]

        causal = _iota((CHUNK, CHUNK), 0) >= _iota((CHUNK, CHUNK), 1)
        for g in range(A_GROUPS):
            wg = jnp.where(causal, ws_ref[g], 0.0).astype(BF16)
            cs = slice(g * gd, (g + 1) * gd)
            chunks = [slice(c * CHUNK, (c + 1) * CHUNK) for c in range(tm // CHUNK)]
            mixed = _dot(wg, jnp.concatenate([v[rs, cs] for rs in chunks], axis=1).astype(BF16))
            for c, rs in enumerate(chunks):
                um_ref[rs, cs] = (u[rs, cs] * (mixed[:, c * gd:(c + 1) * gd] + bias_ref[:, cs])).astype(BF16)
    x1_ref[0] = x + gt_ref[0] * _dot(um_ref[...], wout_ref[...])


def _mixer_a(x, sh, sc, gt, ng, w_in, v_gain, ws, bias, w_out, *, single_token, tm):
    bm, sm, d = x.shape
    width = w_out.shape[0]
    v_rows = sm if single_token else CHUNK
    kern = functools.partial(_mixer_a_kernel, single_token=single_token)
    return pl.pallas_call(
        kern,
        grid=(bm, sm // tm),
        in_specs=[pl.BlockSpec((1, tm, d), lambda b, j: (b, j, 0)),
                  _mod_spec(sh, tm), _mod_spec(sc, tm), _mod_spec(gt, tm),
                  _const_spec(ng.shape), _const_spec(w_in.shape), _const_spec(v_gain.shape),
                  _const_spec(ws.shape), _const_spec(bias.shape), _const_spec(w_out.shape)],
        out_specs=[pl.BlockSpec((1, tm, d), lambda b, j: (b, j, 0)),
                   pl.BlockSpec((1, v_rows, width), lambda b, j: (b, 0, 0))],
        out_shape=[jax.ShapeDtypeStruct((bm, sm, d), F32),
                   jax.ShapeDtypeStruct((bm, v_rows, width), F32)],
        scratch_shapes=[pltpu.VMEM((tm, width), BF16)],
        compiler_params=_params(2),
        name="mixer_a",
    )(x, sh, sc, gt, ng, w_in, v_gain, ws, bias, w_out)


def _ffn_kernel(x_ref, sh_ref, sc_ref, gt_ref, ng_ref, wgu_ref, wd_ref, o_ref, act_ref, *, fc):
    x = x_ref[0]
    h = (_rms(x, ng_ref[...]) * (1.0 + sc_ref[0]) + sh_ref[0]).astype(BF16)
    dff = wd_ref.shape[0]
    for j in range(dff // fc):
        g = _dot(h, wgu_ref[:, j * fc:(j + 1) * fc])
        u = _dot(h, wgu_ref[:, dff + j * fc:dff + (j + 1) * fc])
        act_ref[:, j * fc:(j + 1) * fc] = (g * jax.nn.sigmoid(g) * u).astype(BF16)
    o_ref[0] = x + gt_ref[0] * _dot(act_ref[...], wd_ref[...])


def _ffn(x, sh, sc, gt, ng, w_gu, w_down, *, tm, fc=256):
    bm, sm, d = x.shape
    dff = w_down.shape[0]
    assert dff % fc == 0
    return pl.pallas_call(
        functools.partial(_ffn_kernel, fc=fc),
        grid=(bm, sm // tm),
        in_specs=[pl.BlockSpec((1, tm, d), lambda b, j: (b, j, 0)),
                  _mod_spec(sh, tm), _mod_spec(sc, tm), _mod_spec(gt, tm),
                  _const_spec(ng.shape), _const_spec(w_gu.shape), _const_spec(w_down.shape)],
        out_specs=pl.BlockSpec((1, tm, d), lambda b, j: (b, j, 0)),
        out_shape=jax.ShapeDtypeStruct((bm, sm, d), F32),
        scratch_shapes=[pltpu.VMEM((tm, dff), BF16)],
        compiler_params=_params(2),
        name="swiglu_ffn",
    )(x, sh, sc, gt, ng, w_gu, w_down)


def _head_norm_rope(t, gain_col, cos, sin):
    rows, n = t.shape
    t3 = t.reshape(rows // HEAD_DIM, HEAD_DIM, n)
    y = t3 * lax.rsqrt(jnp.mean(t3 * t3, axis=1, keepdims=True) + EPS) * gain_col[None]
    half = HEAD_DIM // 2
    y1 = y[:, :half, :]
    y2 = y[:, half:, :]
    c = cos[None]
    s = sin[None]
    rot = jnp.concatenate([y1 * c - y2 * s, y2 * c + y1 * s], axis=1)
    return y.reshape(rows, n), rot.reshape(rows, n)


def _kv_proj_kernel(x_ref, kvg_ref, w_ref, g1_ref, g2_ref, cos_ref, sin_ref,
                    kc_ref, vc_ref, ks_ref, vs_ref, kw_ref, vw_ref, ksb_ref, vsb_ref, kwb_ref, vwb_ref):
    xn = _rms(x_ref[0], kvg_ref[...]).astype(BF16)
    kvt = _dot_nt(w_ref[...], xn)
    cos = cos_ref[...]
    sin = sin_ref[...]

    def part(i):
        return kvt[i * KV_WIDTH:(i + 1) * KV_WIDTH]

    kc_ref[0] = part(0)
    vc_ref[0] = part(1)
    _, ks = _head_norm_rope(part(2), g1_ref[...], cos, sin)
    _, kw = _head_norm_rope(part(4), g2_ref[...], cos, sin)
    vs = part(3)
    vw = part(5)
    ks_ref[0] = ks
    vs_ref[0] = vs
    @pl.when(pl.program_id(1) == pl.num_programs(1) - 1)
    def _():
        kw_ref[0] = kw
        vw_ref[0] = vw
    ksb_ref[0] = ks.astype(BF16)
    kwb_ref[0] = kw.astype(BF16)
    ones = jnp.ones((V_AUG - HEAD_DIM, vs.shape[1]), BF16)
    for g in range(KV_HEADS):
        hs = slice(g * HEAD_DIM, (g + 1) * HEAD_DIM)
        vsb_ref[0, g, :HEAD_DIM] = vs[hs].astype(BF16)
        vsb_ref[0, g, HEAD_DIM:] = ones
        vwb_ref[0, g, :HEAD_DIM] = vw[hs].astype(BF16)
        vwb_ref[0, g, HEAD_DIM:] = ones


def _q_proj_kernel(x_ref, sh_ref, sc_ref, ng_ref, w_ref, qg_ref, cos_ref, sin_ref, q_ref, qr_ref, gates_ref):
    h = (_rms(x_ref[0], ng_ref[...]) * (1.0 + sc_ref[0]) + sh_ref[0]).astype(BF16)
    pt = _dot_nt(w_ref[...], h)
    nq = N_HEADS * HEAD_DIM
    qn, qrot = _head_norm_rope(pt[:nq], qg_ref[...], cos_ref[...], sin_ref[...])
    scale = LOG2E * HEAD_DIM ** -0.5
    q_ref[0] = (qn * scale).T.astype(BF16)
    qr_ref[0] = (qrot * scale).T.astype(BF16)
    gates_ref[0] = jax.nn.sigmoid(pt[nq:]).T


N_KV_OUTS = 10


def _kvq_proj_kernel(x_ref, kvg_ref, wkv_ref, g1_ref, g2_ref, cos_ref, sin_ref, sh_ref, sc_ref, ng_ref, wq_ref,
                     qg_ref, *outs):
    _kv_proj_kernel(x_ref, kvg_ref, wkv_ref, g1_ref, g2_ref, cos_ref, sin_ref, *outs[:N_KV_OUTS])
    _q_proj_kernel(x_ref, sh_ref, sc_ref, ng_ref, wq_ref, qg_ref, cos_ref, sin_ref, *outs[N_KV_OUTS:])


def _kvq_proj(x, kv_gain, w_kvt, g1, g2, cos_t, sin_t, sh, sc, ng, w_int, q_gain, *, tm):
    bm, sm, d = x.shape
    nq = N_HEADS * HEAD_DIM
    half = HEAD_DIM // 2
    rope_spec = pl.BlockSpec((half, tm), lambda b, j: (0, j))
    out_spec = pl.BlockSpec((1, KV_WIDTH, tm), lambda b, j: (b, 0, j))
    aug_spec = pl.BlockSpec((1, KV_HEADS, V_AUG, tm), lambda b, j: (b, 0, 0, j))
    last_spec = pl.BlockSpec((1, KV_WIDTH, tm), lambda b, j: (b, 0, 0))
    q_spec = pl.BlockSpec((1, tm, nq), lambda b, j: (b, j, 0))
    k_bf = jax.ShapeDtypeStruct((bm, KV_WIDTH, sm), BF16)
    v_bf = jax.ShapeDtypeStruct((bm, KV_HEADS, V_AUG, sm), BF16)
    return pl.pallas_call(
        _kvq_proj_kernel,
        grid=(bm, sm // tm),
        in_specs=[pl.BlockSpec((1, tm, d), lambda b, j: (b, j, 0)),
                  _const_spec(kv_gain.shape), _const_spec(w_kvt.shape), _const_spec(g1.shape),
                  _const_spec(g2.shape), rope_spec, rope_spec,
                  _mod_spec(sh, tm), _mod_spec(sc, tm),
                  _const_spec(ng.shape), _const_spec(w_int.shape), _const_spec(q_gain.shape)],
        out_specs=[out_spec] * 4 + [last_spec] * 2 + [out_spec, aug_spec, out_spec, aug_spec]
                  + [q_spec, q_spec, pl.BlockSpec((1, tm, GATE_PAD), lambda b, j: (b, j, 0))],
        out_shape=[jax.ShapeDtypeStruct((bm, KV_WIDTH, sm), F32)] * 4
                  + [jax.ShapeDtypeStruct((bm, KV_WIDTH, tm), F32)] * 2 + [k_bf, v_bf, k_bf, v_bf]
                  + [jax.ShapeDtypeStruct((bm, sm, nq), BF16), jax.ShapeDtypeStruct((bm, sm, nq), BF16),
                     jax.ShapeDtypeStruct((bm, sm, GATE_PAD), F32)],
        compiler_params=_params(2),
        name="kvq_projection",
    )(x, kv_gain, w_kvt, g1, g2, cos_t, sin_t, sh, sc, ng, w_int, q_gain)


def _store_rows(xs_ref, xt):
    for c in range(xs_ref.shape[0]):
        xs_ref[c] = xt[c * LANES:(c + 1) * LANES, :].T


def _compress_halves(xs_ref, n, pe_a, pe_b, w1a_ref, w1b_ref):
    low = _iota((n, LANES), 1) < HEAD_DIM
    per_head = [[] for _ in range(KV_HEADS)]
    for c in range(KV_WIDTH // LANES):
        for r in range(0, CMP_STRIDE, 2):
            x0 = xs_ref[c, pl.ds(r, n, stride=CMP_STRIDE), :]
            x1 = xs_ref[c, pl.ds(r + 1, n, stride=CMP_STRIDE), :]
            per_head[2 * c].append(jnp.where(low, x0, pltpu.roll(x1, shift=HEAD_DIM, axis=1)))
            per_head[2 * c + 1].append(jnp.where(low, pltpu.roll(x0, shift=HEAD_DIM, axis=1), x1))
    lhs = jnp.concatenate([jnp.concatenate(p, axis=1) for p in per_head], axis=0)
    a = _dot((lhs + pe_a).astype(BF16), w1a_ref[...])
    b = _dot((lhs + pe_b).astype(BF16), w1b_ref[...])
    return a, b


def _compress_finish(a, b_next, b1, w2_ref, b2, gain):
    hid = jax.nn.gelu(a + b_next + b1)
    out = _dot(hid.astype(BF16), w2_ref[...]) + b2
    if gain is not None:
        ms = jnp.sum(out * out, axis=-1, keepdims=True) * (1.0 / HEAD_DIM)
        out = out * lax.rsqrt(ms + EPS) * gain
    return out


def _compress_kernel(x_ref, nxt_ref, pea_ref, peb_ref, w1a_ref, w1b_ref, b1_ref, w2_ref, b2_ref, gain_ref,
                     o_ref, xs_ref, xn_ref, *, norm):
    lc = x_ref.shape[-1]
    n = lc // CMP_STRIDE
    nn = nxt_ref.shape[-1] // CMP_STRIDE
    _store_rows(xs_ref, x_ref[0])
    _store_rows(xn_ref, nxt_ref[0])
    a, b = _compress_halves(xs_ref, n, pea_ref[...], peb_ref[...], w1a_ref, w1b_ref)
    _, bn = _compress_halves(xn_ref, nn, pea_ref[...], peb_ref[...], w1a_ref, w1b_ref)
    last_row = _iota((n, 1), 0) == n - 1
    b_next = jnp.concatenate(
        [jnp.where(last_row, bn[g * nn:g * nn + 1, :], pltpu.roll(b[g * n:(g + 1) * n], shift=n - 1, axis=0))
         for g in range(KV_HEADS)], axis=0)
    out = _compress_finish(a, b_next, b1_ref[...], w2_ref, b2_ref[...], gain_ref[...] if norm else None)
    for g in range(KV_HEADS):
        o_ref[0, g] = out[g * n:(g + 1) * n].T[:HEAD_DIM]


def _compress_prompt(xt, cw, *, norm, lc=2048, ln=256):
    b, _, s = xt.shape
    lc = min(lc, s)
    n = lc // CMP_STRIDE
    n_next = s // ln
    return pl.pallas_call(
        functools.partial(_compress_kernel, norm=norm),
        grid=(b, s // lc),
        in_specs=[pl.BlockSpec((1, KV_WIDTH, lc), lambda i, c: (i, 0, c)),
                  pl.BlockSpec((1, KV_WIDTH, ln),
                               lambda i, c: (i, 0, jnp.minimum((c + 1) * (lc // ln), n_next - 1)))]
                 + [_const_spec(w.shape) for w in cw],
        out_specs=pl.BlockSpec((1, KV_HEADS, HEAD_DIM, n), lambda i, c: (i, 0, 0, c)),
        out_shape=jax.ShapeDtypeStruct((b, KV_HEADS, HEAD_DIM, s // CMP_STRIDE), F32),
        scratch_shapes=[pltpu.VMEM((KV_WIDTH // LANES, lc, LANES), F32),
                        pltpu.VMEM((KV_WIDTH // LANES, ln, LANES), F32)],
        compiler_params=_params(2),
        name="compress_prompt",
    )(xt, xt, *cw)


def _select_blocks(scores_t, n_sel):
    srow = _iota(scores_t[0].shape, 0).astype(F32)
    n_blocks = float(scores_t[0].shape[0])
    vals = list(scores_t)
    for _ in range(n_sel):
        for g, v in enumerate(vals):
            best = jnp.max(v, axis=0, keepdims=True)
            idx = jnp.min(jnp.where(v == best, srow, n_blocks), axis=0, keepdims=True)
            vals[g] = jnp.where(srow == idx, KNOCKED_OUT, v)
    return [jnp.where(v < TAKEN_BELOW, 1.0, 0.0) for v in vals]


def _attn_prompt_kernel(q_ref, qr_ref, gates_ref, x_ref, g1_ref, kct_ref, vct_ref, ks_ref, vs_ref, kw_ref,
                        vw_ref, nege_ref, ov_ref, wo_ref, o_ref, *, nc, n_sel, tk):
    n = pl.program_id(1)
    qb = Q_BLOCK
    hq = HEADS_PER_GROUP * qb
    q0 = n * qb
    ncp = kct_ref.shape[-1]
    qpos = q0 + _iota((qb, 1), 0)
    cur = qpos // SLC_BLOCK
    gates = gates_ref[0]
    q_all = q_ref[0]
    qr_all = qr_ref[0]

    ci = _iota((qb, ncp), 1)
    vis = (ci < nc) & (ci * CMP_STRIDE + (CMP_BLOCK - 1) <= qpos)
    vis_bias = jnp.where(vis, 0.0, NEG_INF)
    row_visible = jnp.where((qpos >= CMP_BLOCK - 1) & (nc > 0), 1.0, 0.0)
    blk = _iota((qb, LANES), 1)
    valid = blk <= cur
    forced = (blk == 0) | (blk == cur) | (blk == cur - 1)
    n_tiles = (q0 + qb + tk - 1) // tk

    def group_rows(x_all, g):
        return jnp.concatenate([x_all[:, h * HEAD_DIM:(h + 1) * HEAD_DIM]
                                for h in range(g * HEADS_PER_GROUP, (g + 1) * HEADS_PER_GROUP)], axis=0)

    ocs, imps = [], []
    for g in range(KV_HEADS):
        s = _dot(group_rows(q_all, g), kct_ref[0, g].astype(BF16)).reshape(HEADS_PER_GROUP, qb, ncp)
        s = s + vis_bias[None]
        p = jnp.exp2(s - jnp.max(s, axis=-1, keepdims=True))
        pc = p * (row_visible[None] / jnp.maximum(jnp.sum(p, axis=-1, keepdims=True), TINY))
        ocs.append(_dot_nt(pc.reshape(hq, ncp).astype(BF16), vct_ref[0, g].astype(BF16)))
        imps.append(_split_dot(pc[0] + pc[1] + pc[2] + pc[3], ov_ref[...]))

    k_last = (n_tiles - 1) * tk
    causal_bias = jnp.where(k_last + _iota((qb, tk), 1) <= qpos, 0.0, NEG_INF)
    w_tiles = WINDOW // qb + 1
    w_offs, w_biases = [], []
    for t in range(w_tiles):
        off = q0 - WINDOW + qb * t
        w_offs.append(pl.multiple_of(jnp.maximum(off, 0), qb))
        wpos = off + _iota((qb, qb), 1)
        dpos = qpos - wpos
        w_biases.append(jnp.where((dpos >= 0) & (dpos <= WINDOW) & (wpos >= 0), 0.0, NEG_INF))
    w_bias = jnp.concatenate(w_biases, axis=1)
    wk = w_tiles * qb

    qrgs = [group_rows(qr_all, g) for g in range(KV_HEADS)]
    acc_ws = []
    for g in range(KV_HEADS):
        hs = slice(g * HEAD_DIM, (g + 1) * HEAD_DIM)
        kwt = jnp.concatenate([kw_ref[0, hs, pl.ds(o, qb)] for o in w_offs], axis=1)
        vwt = jnp.concatenate([vw_ref[0, g, :, pl.ds(o, qb)] for o in w_offs], axis=1)
        sw = (_dot(qrgs[g], kwt).reshape(HEADS_PER_GROUP, qb, wk) + w_bias[None]).reshape(hq, wk)
        pw = jnp.exp2(sw - jnp.max(sw, axis=-1, keepdims=True)).astype(BF16)
        acc_ws.append(_dot_nt(pw, vwt))

    scores = [jnp.where(valid, jnp.where(forced, FORCED_MARK, imp), NEG_INF).T for imp in imps]
    sel_t = _select_blocks(scores, n_sel - 3)
    forced_one = jnp.where(forced, 1.0, 0.0)

    lhss = []
    for g in range(KV_HEADS):
        not_sel = jnp.where(valid, 1.0 - jnp.maximum(sel_t[g].T, forced_one), 1.0).astype(BF16)
        lhss.append(jnp.concatenate([jnp.concatenate([not_sel] * HEADS_PER_GROUP, axis=0), qrgs[g]], axis=1))

    def tile(k0, width, carry, causal):
        neg = nege_ref[:, pl.ds(k0, width)]
        out = []
        for g in range(KV_HEADS):
            m_i, acc = carry[g]
            rhs = jnp.concatenate([neg, ks_ref[0, g * HEAD_DIM:(g + 1) * HEAD_DIM, pl.ds(k0, width)]], axis=0)
            st = _dot(lhss[g], rhs)
            if causal:
                st = (st.reshape(HEADS_PER_GROUP, qb, width) + causal_bias[None]).reshape(hq, width)
            m_new = jnp.maximum(m_i, jnp.max(st, axis=-1, keepdims=True))
            pt = jnp.exp2(st - m_new).astype(BF16)
            acc = jnp.exp2(m_i - m_new) * acc + _dot_nt(pt, vs_ref[0, g, :, pl.ds(k0, width)])
            out.append((m_new, acc))
        return tuple(out)

    n_full = n_tiles - 1
    one = n_full % 2
    two = (n_full // 2) % 2
    init = tuple((jnp.full((hq, 1), NEG_INF, F32), jnp.zeros((hq, V_AUG), F32)) for _ in range(KV_HEADS))
    carry = lax.cond(one == 1, lambda c: tile(0, tk, c, False), lambda c: c, init)
    carry = lax.cond(two == 1, lambda c: tile(pl.multiple_of(one * tk, tk), 2 * tk, c, False), lambda c: c, carry)
    carry = lax.fori_loop(
        0, n_full // 4,
        lambda j, c: tile(pl.multiple_of((one + 2 * two + 4 * j) * tk, tk), 4 * tk, c, False), carry)
    accs = [c[1] for c in tile(pl.multiple_of(n_full * tk, tk), tk, carry, True)]

    proj = jnp.zeros((qb, wo_ref.shape[-1]), F32)
    for g in range(KV_HEADS):
        acc_s = accs[g]
        acc_w = acc_ws[g]
        for i in range(HEADS_PER_GROUP):
            h = g * HEADS_PER_GROUP + i
            rs = slice(i * qb, (i + 1) * qb)
            gs = gates[:, 3 * h + 1:3 * h + 2] / jnp.maximum(acc_s[rs, HEAD_DIM:HEAD_DIM + 1], TINY)
            gw = gates[:, 3 * h + 2:3 * h + 3] / jnp.maximum(acc_w[rs, HEAD_DIM:HEAD_DIM + 1], TINY)
            o_h = (ocs[g][rs] * gates[:, 3 * h:3 * h + 1] + acc_s[rs, :HEAD_DIM] * gs
                   + acc_w[rs, :HEAD_DIM] * gw)
            proj = proj + _dot(o_h.astype(BF16), wo_ref[h])
    o_ref[0] = x_ref[0] + g1_ref[0] * proj


def _attn_prompt(q, qr, gates, x, gate1, kct, vct, ksb, vsb, kwb, vwb, nege, ov, wo, *, tk=512):
    b, s, d = x.shape
    nq = q.shape[-1]
    ncp = kct.shape[-1]
    nc = s // CMP_STRIDE - 1
    ns = -(-s // SLC_BLOCK)
    tk = min(tk, s)
    kern = functools.partial(_attn_prompt_kernel, nc=nc, n_sel=min(N_SELECT, ns), tk=tk)
    full_k = pl.BlockSpec((1, KV_WIDTH, s), lambda i, j: (i, 0, 0), pipeline_mode=pl.Buffered(1))
    full_v = pl.BlockSpec((1, KV_HEADS, V_AUG, s), lambda i, j: (i, 0, 0, 0), pipeline_mode=pl.Buffered(1))
    cmp_spec = pl.BlockSpec((1, KV_HEADS, HEAD_DIM, ncp), lambda i, j: (i, 0, 0, 0))
    return pl.pallas_call(
        kern,
        grid=(b, s // Q_BLOCK),
        in_specs=[pl.BlockSpec((1, Q_BLOCK, nq), lambda i, j: (i, j, 0)),
                  pl.BlockSpec((1, Q_BLOCK, nq), lambda i, j: (i, j, 0)),
                  pl.BlockSpec((1, Q_BLOCK, GATE_PAD), lambda i, j: (i, j, 0)),
                  pl.BlockSpec((1, Q_BLOCK, d), lambda i, j: (i, j, 0)),
                  pl.BlockSpec((1, 1, d), lambda i, j: (i, 0, 0)),
                  cmp_spec, cmp_spec, full_k, full_v, full_k, full_v,
                  pl.BlockSpec(nege.shape, lambda i, j: (0, 0), pipeline_mode=pl.Buffered(1)),
                  _const_spec(ov.shape), _const_spec(wo.shape)],
        out_specs=pl.BlockSpec((1, Q_BLOCK, d), lambda i, j: (i, j, 0)),
        out_shape=jax.ShapeDtypeStruct((b, s, d), F32),
        compiler_params=_params(2),
        name="nsa_prompt",
    )(q, qr, gates, x, gate1, kct, vct, ksb, vsb, kwb, vwb, nege, ov, wo)


def _attn_sample_kernel(pt_ref, ck_ref, cv_ref, sk_ref, sv_ref, wk_ref, wv_ref, q_ref, qr_ref, gates_ref,
                        ksn_ref, vsn_ref, kwn_ref, vwn_ref, kwc_ref, vwc_ref,
                        kpea_ref, kpeb_ref, kw1a_ref, kw1b_ref, kb1_ref, kw2_ref, kb2_ref, kgain_ref,
                        vpea_ref, vpeb_ref, vw1a_ref, vw1b_ref, vb1_ref, vw2_ref, vb2_ref, vgain_ref,
                        e_ref, ov_ref,
                        o_ref, wko_ref, wvo_ref,
                        slabs, xs_ref, sems, *, past, page, n_sel):
    del vgain_ref
    b = pl.program_id(0)
    slot = b % 2
    caches = (ck_ref, cv_ref, sk_ref, sv_ref)

    def page_copy(seq, slot_, t, p):
        return pltpu.make_async_copy(caches[t].at[pt_ref[seq, p]],
                                     slabs.at[slot_, t, :, pl.ds(p * page, page)], sems.at[slot_, t])

    def gather(seq, slot_, start):
        for t in range(len(caches)):
            for p in range(past // page):
                cp = page_copy(seq, slot_, t, p)
                cp.start() if start else cp.wait()

    @pl.when(b == 0)
    def _():
        gather(0, 0, True)

    @pl.when(b + 1 < pl.num_programs(0))
    def _():
        gather(b + 1, 1 - slot, True)

    gather(b, slot, False)
    sl_ck, sl_cv, sl_sk, sl_sv = (slabs.at[slot, t] for t in range(len(caches)))

    def compute():
        nh = N_HEADS
        qpos = past
        n = past // CMP_STRIDE
        nc = n - 1
        cur = qpos // SLC_BLOCK
        win = wk_ref.shape[-1]
        hrow = _iota((nh, 1), 0) // HEADS_PER_GROUP

        def by_group(fn, width):
            out = jnp.zeros((nh, width), F32)
            for g in range(KV_HEADS):
                out = jnp.where(hrow == g, fn(g), out)
            return out

        def compress(slab, pea, peb, w1a, w1b, b1, w2, b2, gain):
            _store_rows(xs_ref, slab[...])
            a, bh = _compress_halves(xs_ref, n, pea[...], peb[...], w1a, w1b)
            b_next = jnp.concatenate([pltpu.roll(bh[g * n:(g + 1) * n], shift=n - 1, axis=0)
                                      for g in range(KV_HEADS)], axis=0)
            out = _compress_finish(a, b_next, b1[...], w2, b2[...], gain)
            return [out[g * n:(g + 1) * n] for g in range(KV_HEADS)]

        kc = compress(sl_ck, kpea_ref, kpeb_ref, kw1a_ref, kw1b_ref, kb1_ref, kw2_ref, kb2_ref, kgain_ref[...])
        vc = compress(sl_cv, vpea_ref, vpeb_ref, vw1a_ref, vw1b_ref, vb1_ref, vw2_ref, vb2_ref, None)

        q = q_ref[0]
        qr = qr_ref[0]
        qr_lo = qr[:, :HEAD_DIM]
        qr_f = qr_lo.astype(F32)

        sc = by_group(lambda g: _dot_nt(q, kc[g].astype(BF16)), n)
        ci = _iota((nh, n), 1)
        vis = (ci < nc) & (ci * CMP_STRIDE + (CMP_BLOCK - 1) <= qpos)
        sc = jnp.where(vis, sc, NEG_INF)
        pc = jnp.where(vis, jnp.exp2(sc - jnp.max(sc, axis=-1, keepdims=True)), 0.0)
        pc = pc / jnp.maximum(jnp.sum(pc, axis=-1, keepdims=True), TINY)
        o_c = by_group(lambda g: _dot(pc.astype(BF16), vc[g].astype(BF16)), LANES)[:, :HEAD_DIM]

        grow = _iota((nh, 1), 0)
        psum = jnp.zeros((nh, n), F32)
        for g in range(KV_HEADS):
            pg = jnp.sum(pc[g * HEADS_PER_GROUP:(g + 1) * HEADS_PER_GROUP], axis=0, keepdims=True)
            psum = jnp.where(grow == g, pg, psum)
        imp = _split_dot(psum, ov_ref[...])
        blk = _iota((nh, LANES), 1)
        valid = blk <= cur
        forced = (blk == 0) | (blk == cur) | (blk == cur - 1)
        score = jnp.where(valid, jnp.where(forced, FORCE_SCORE, imp), NEG_INF)
        rank = jnp.zeros((nh, LANES), F32)
        for sp in range(cur + 1):
            col = score[:, sp:sp + 1]
            rank = rank + jnp.where(col > score, 1.0, jnp.where(col == score, jnp.where(sp < blk, 1.0, 0.0), 0.0))
        sel_g = jnp.where(valid, jnp.where(rank < n_sel, 1.0, 0.0), 0.0)
        sel = by_group(lambda g: sel_g[g:g + 1, :], LANES)

        chosen = _dot(sel.astype(BF16), e_ref[...])
        kpos = _iota((nh, past), 1)
        bias = jnp.where(kpos <= qpos, jnp.where(chosen > 0.5, 0.0, NEG_INF), NEG_INF)
        ss = by_group(lambda g: _dot(qr_lo, sl_sk[g * HEAD_DIM:(g + 1) * HEAD_DIM, :].astype(BF16)), past) + bias
        s_new = jnp.sum(qr_f * ksn_ref[0], axis=-1, keepdims=True)
        s_new = jnp.where(sel[:, cur:cur + 1] > 0.5, s_new, NEG_INF)
        m = jnp.maximum(jnp.max(ss, axis=-1, keepdims=True), s_new)
        ps = jnp.exp2(ss - m)
        p_new = jnp.exp2(s_new - m)
        den = jnp.maximum(jnp.sum(ps, axis=-1, keepdims=True) + p_new, TINY)
        psb = ps.astype(BF16)
        o_s = by_group(lambda g: _dot_nt(psb, sl_sv[g * HEAD_DIM:(g + 1) * HEAD_DIM, :].astype(BF16)), HEAD_DIM)
        o_s = (o_s + p_new * vsn_ref[0]) / den

        wpos = qpos - win + _iota((nh, win), 1)
        dpos = qpos - wpos
        okw = (dpos >= 0) & (dpos <= WINDOW) & (wpos >= 0)
        sw = by_group(lambda g: _dot(qr_lo, wk_ref[0, g * HEAD_DIM:(g + 1) * HEAD_DIM, :].astype(BF16)), win)
        sw = sw + jnp.where(okw, 0.0, NEG_INF)
        sw_new = jnp.sum(qr_f * kwn_ref[0], axis=-1, keepdims=True)
        mw = jnp.maximum(jnp.max(sw, axis=-1, keepdims=True), sw_new)
        pw = jnp.exp2(sw - mw)
        pw_new = jnp.exp2(sw_new - mw)
        denw = jnp.maximum(jnp.sum(pw, axis=-1, keepdims=True) + pw_new, TINY)
        pwb = pw.astype(BF16)
        o_w = by_group(lambda g: _dot_nt(pwb, wv_ref[0, g * HEAD_DIM:(g + 1) * HEAD_DIM, :].astype(BF16)), HEAD_DIM)
        o_w = (o_w + pw_new * vwn_ref[0]) / denw

        gates = gates_ref[0]
        o_ref[0] = o_c * gates[:, 0:1] + o_s * gates[:, 1:2] + o_w * gates[:, 2:3]

        lane = _iota((KV_WIDTH, win), 1)
        wko_ref[0] = jnp.where(lane == win - 1, kwc_ref[0], pltpu.roll(wk_ref[0], shift=win - 1, axis=1))
        wvo_ref[0] = jnp.where(lane == win - 1, vwc_ref[0], pltpu.roll(wv_ref[0], shift=win - 1, axis=1))

    compute()


def _attn_sample(page_table, caches, win_k, win_v, q, qr, gates, new16, new_cols, cw_k, cw_v, e_mat, ov):
    bd, n_pages = page_table.shape
    page = caches[0].shape[-1]
    past = n_pages * page
    win = win_k.shape[-1]
    cur = past // SLC_BLOCK
    kern = functools.partial(_attn_sample_kernel, past=past, page=page, n_sel=min(N_SELECT, cur + 1))

    def per_seq(a):
        nd = a.ndim
        return pl.BlockSpec((1,) + a.shape[1:], lambda b, pt: (b,) + (0,) * (nd - 1))

    def const(a):
        nd = a.ndim
        return pl.BlockSpec(a.shape, lambda b, pt: (0,) * nd)

    consts = list(cw_k) + list(cw_v) + [e_mat, ov]
    grid_spec = pltpu.PrefetchScalarGridSpec(
        num_scalar_prefetch=1,
        grid=(bd,),
        in_specs=[pl.BlockSpec(memory_space=pl.ANY)] * len(caches)
                 + [per_seq(win_k), per_seq(win_v), per_seq(q), per_seq(qr), per_seq(gates)]
                 + [per_seq(a) for a in new16] + [per_seq(a) for a in new_cols] + [const(a) for a in consts],
        out_specs=[pl.BlockSpec((1, N_HEADS, HEAD_DIM), lambda b, pt: (b, 0, 0)),
                   pl.BlockSpec((1, KV_WIDTH, win), lambda b, pt: (b, 0, 0)),
                   pl.BlockSpec((1, KV_WIDTH, win), lambda b, pt: (b, 0, 0))],
        scratch_shapes=[pltpu.VMEM((2, len(caches), KV_WIDTH, past), F32),
                        pltpu.VMEM((KV_WIDTH // LANES, past, LANES), F32),
                        pltpu.SemaphoreType.DMA((2, len(caches)))],
    )
    return pl.pallas_call(
        kern,
        grid_spec=grid_spec,
        out_shape=[jax.ShapeDtypeStruct((bd, N_HEADS, HEAD_DIM), F32),
                   jax.ShapeDtypeStruct((bd, KV_WIDTH, win), F32),
                   jax.ShapeDtypeStruct((bd, KV_WIDTH, win), F32)],
        compiler_params=_params(1),
        name="nsa_decode",
    )(page_table, *caches, win_k, win_v, q, qr, gates, *new16, *new_cols, *consts)


def _out_proj_kernel(o_ref, x_ref, g_ref, w_ref, y_ref):
    y_ref[0] = x_ref[0] + g_ref[0] * _dot(o_ref[0].astype(BF16), w_ref[...])


def _out_proj(o, x, gate, w):
    bm, sm, d = x.shape
    spec = pl.BlockSpec((1, sm, d), lambda b: (b, 0, 0))
    return pl.pallas_call(
        _out_proj_kernel,
        grid=(bm,),
        in_specs=[pl.BlockSpec((1, sm, o.shape[-1]), lambda b: (b, 0, 0)), spec, spec, _const_spec(w.shape)],
        out_specs=spec,
        out_shape=jax.ShapeDtypeStruct((bm, sm, d), F32),
        compiler_params=_params(1),
        name="attn_out_proj",
    )(o, x, gate, w)


def _rope_tables(pos):
    half = HEAD_DIM // 2
    inv_freq = ROPE_THETA ** (-jnp.arange(half, dtype=F32) / half)
    ang = inv_freq[:, None] * pos.astype(F32)[None, :]
    return jnp.cos(ang), jnp.sin(ang)


def _overlap_matrix(n_rows, nc):
    i = np.arange(n_rows)[:, None] * CMP_STRIDE
    s = np.arange(LANES)[None, :] * SLC_BLOCK
    ov = (i < s + SLC_BLOCK) & (i + CMP_BLOCK > s) & (np.arange(n_rows)[:, None] < nc)
    return jnp.asarray(ov.astype(np.float32), dtype=BF16)


def _expand_matrix(n_keys):
    e = (np.arange(n_keys)[None, :] // SLC_BLOCK) == np.arange(LANES)[:, None]
    return jnp.asarray(e.astype(np.float32), dtype=BF16)


def _compress_weights(t, cmp_pe, cmp_w1, cmp_b1, cmp_w2, cmp_b2, gain):
    half = CMP_STRIDE * HEAD_DIM
    pad = LANES - HEAD_DIM
    return (cmp_pe[t, :CMP_STRIDE].reshape(1, half), cmp_pe[t, CMP_STRIDE:].reshape(1, half),
            cmp_w1[t, :half].astype(BF16), cmp_w1[t, half:].astype(BF16), cmp_b1[t][None, :],
            jnp.pad(cmp_w2[t], ((0, 0), (0, pad))).astype(BF16), jnp.pad(cmp_b2[t], (0, pad))[None, :],
            jnp.pad(gain, (0, pad))[None, :])


def _to_rows(t):
    b, _, n = t.shape
    return t.reshape(b, KV_HEADS, HEAD_DIM, n).transpose(0, 3, 1, 2)


def kernel(x_prompt, x_sample, cache_cmp_k, cache_cmp_v, cache_slc_k, cache_slc_v, state_win_k, state_win_v,
           page_table, c_prompt, c_sample, w_ada, b_ada, norm_g, a_w_in, a_v_gain, a_w_s, a_b_s, a_w_out,
           b_w_in, b_q_gain, b_w_out, kv_gain, w_kv, k_gains, cmp_pe, cmp_w1, cmp_b1, cmp_w2, cmp_b2,
           ffn_w_gu, ffn_w_down):
    b, s, d = x_prompt.shape
    bd = x_sample.shape[0]
    assert x_sample.shape[1] == 1 and w_ada.shape[0] == 2 and a_w_in.shape[0] == 1 and b_w_in.shape[0] == 1
    n_phys, page = cache_cmp_k.shape[:2]
    past = page_table.shape[1] * page
    win = state_win_k.shape[1]
    assert win == WINDOW and s % 512 == 0 and s >= WINDOW and past % SLC_BLOCK == 0
    width = a_w_out.shape[1]
    gd = width // A_GROUPS

    rows = b + bd
    rows_p = -(-rows // 8) * 8
    c_all = jnp.pad(jnp.concatenate([c_prompt, c_sample], axis=0), ((0, rows_p - rows), (0, 0)))
    mod = _ada(c_all, w_ada, b_ada)

    def mods(layer, lo, hi, per_token):
        m = mod[layer, lo:hi].reshape(hi - lo, 6, d)
        return [m[None, :, k, :] if per_token else m[:, None, k, :] for k in range(6)]

    w_in_a = a_w_in[0].astype(BF16)
    w_out_a = a_w_out[0].astype(BF16)
    vg = a_v_gain[0][None, :]
    bias_p = jnp.repeat(a_b_s[0].T, gd, axis=1)
    diag_s = jnp.repeat(a_w_s[0, :, 0, 0], gd)[None, :]
    bias_s = jnp.repeat(a_b_s[0, :, 0], gd)[None, :]
    w_gu = ffn_w_gu.astype(BF16)
    w_down = ffn_w_down.astype(BF16)
    w_kvt = w_kv.T.astype(BF16)
    w_qt = jnp.pad(b_w_in[0].T, ((0, GATE_PAD - 3 * N_HEADS), (0, 0))).astype(BF16)
    w_o = b_w_out[0].astype(BF16)
    kg1 = k_gains[1][:, None]
    kg2 = k_gains[2][:, None]
    qg = b_q_gain[0][:, None]
    cw_k = _compress_weights(0, cmp_pe, cmp_w1, cmp_b1, cmp_w2, cmp_b2, k_gains[0])
    cw_v = _compress_weights(1, cmp_pe, cmp_w1, cmp_b1, cmp_w2, cmp_b2, k_gains[0])
    ng = norm_g[:, :, None, :]

    tm = WINDOW
    sh1, sc1, gt1, sh2, sc2, gt2 = mods(0, 0, b, False)
    x1, v_p = _mixer_a(x_prompt, sh1, sc1, gt1, ng[0, 0], w_in_a, vg, a_w_s[0], bias_p, w_out_a,
                       single_token=False, tm=tm)
    x2 = _ffn(x1, sh2, sc2, gt2, ng[0, 1], w_gu[0], w_down[0], tm=tm)

    cos_p, sin_p = _rope_tables(jnp.arange(s, dtype=jnp.int32))
    sh1, sc1, gt1, sh2, sc2, gt2 = mods(1, 0, b, False)
    kct_p, vct_p, kst_p, vst_p, kwt_p, vwt_p, ksb, vsb, kwb, vwb, q_p, qr_p, gates_p = _kvq_proj(
        x2, kv_gain[None, :], w_kvt, kg1, kg2, cos_p, sin_p, sh1, sc1, ng[1, 0], w_qt, qg, tm=tm)
    kcc = _compress_prompt(kct_p, cw_k, norm=True)
    vcc = _compress_prompt(vct_p, cw_v, norm=False)
    ncp = s // CMP_STRIDE
    x3 = _attn_prompt(q_p, qr_p, gates_p, x2, gt1, kcc, vcc, ksb, vsb, kwb, vwb,
                      _expand_matrix(s) * NEG_INF, _overlap_matrix(ncp, ncp - 1),
                      w_o.reshape(N_HEADS, HEAD_DIM, d))
    y_prompt = _ffn(x3, sh2, sc2, gt2, ng[1, 1], w_gu[1], w_down[1], tm=tm)

    xs = x_sample.reshape(1, bd, d)
    sh1, sc1, gt1, sh2, sc2, gt2 = mods(0, b, b + bd, True)
    xs1, v_s = _mixer_a(xs, sh1, sc1, gt1, ng[0, 0], w_in_a, vg, diag_s, bias_s, w_out_a,
                        single_token=True, tm=bd)
    xs2 = _ffn(xs1, sh2, sc2, gt2, ng[0, 1], w_gu[0], w_down[0], tm=bd)

    cos_s, sin_s = _rope_tables(jnp.full((bd,), past, dtype=jnp.int32))
    sh1, sc1, gt1, sh2, sc2, gt2 = mods(1, b, b + bd, True)
    proj_s = _kvq_proj(xs2, kv_gain[None, :], w_kvt, kg1, kg2, cos_s, sin_s, sh1, sc1, ng[1, 0], w_qt, qg, tm=bd)
    new_t = proj_s[:6]
    q_s, qr_s, gates_s = proj_s[N_KV_OUTS:]

    def per_head(t):
        r = t[0].T.reshape(bd, KV_HEADS, 1, HEAD_DIM)
        return jnp.broadcast_to(r, (bd, KV_HEADS, HEADS_PER_GROUP, HEAD_DIM)).reshape(bd, N_HEADS, HEAD_DIM)

    def lane_pad(t):
        return jnp.pad(t.reshape(bd, N_HEADS, HEAD_DIM), ((0, 0), (0, 0), (0, LANES - HEAD_DIM)))

    caches = [c.transpose(0, 2, 3, 1).reshape(n_phys, KV_WIDTH, page)
              for c in (cache_cmp_k, cache_cmp_v, cache_slc_k, cache_slc_v)]
    win_k = state_win_k.transpose(0, 2, 3, 1).reshape(bd, KV_WIDTH, win)
    win_v = state_win_v.transpose(0, 2, 3, 1).reshape(bd, KV_WIDTH, win)
    new16 = [per_head(new_t[i]) for i in (2, 3, 4, 5)]
    new_cols = [new_t[i][0].T[:, :, None] for i in (4, 5)]
    gates16 = gates_s[0, :, :3 * N_HEADS].reshape(bd, N_HEADS, 3)
    o_s, wk_new, wv_new = _attn_sample(
        page_table, caches, win_k, win_v, lane_pad(q_s), lane_pad(qr_s), gates16, new16, new_cols,
        cw_k, cw_v, _expand_matrix(past), _overlap_matrix(past // CMP_STRIDE, past // CMP_STRIDE - 1))
    xs3 = _out_proj(o_s.reshape(1, bd, N_HEADS * HEAD_DIM), xs2, gt1, w_o)
    y_sample = _ffn(xs3, sh2, sc2, gt2, ng[1, 1], w_gu[1], w_down[1], tm=bd)

    new_rows =[t[0].T.reshape(bd, 1, KV_HEADS, HEAD_DIM) for t in new_t[:4]]
    return (y_prompt, y_sample.reshape(bd, 1, d), v_p[None], v_s.reshape(1, bd, 1, width),
            _to_rows(kct_p), _to_rows(vct_p), _to_rows(kst_p), _to_rows(vst_p),
            _to_rows(kwt_p), _to_rows(vwt_p),
            new_rows[0], new_rows[1], new_rows[2], new_rows[3],
            _to_rows(wk_new), _to_rows(wv_new))
```
